```python
import math
import jax, jax.numpy as jnp
from jax import lax
import numpy as np

D_MODEL = 1024
BATCH = 1
SEQ = 16384
DEPTH = 1
DEC_BATCH = 8
DEC_SEQ = 4096
PAST_LEN = 128

HEAD_DIM = 64
N_HEADS = D_MODEL // HEAD_DIM
N_KV_HEADS = N_HEADS // 4
GQA_GROUP = N_HEADS // N_KV_HEADS
ATTN_WIDTH = N_HEADS * HEAD_DIM
KV_WIDTH = N_KV_HEADS * HEAD_DIM
CONV_WIDTH = D_MODEL
CONV_K = 3
GRID_W = 64
Q_BLOCK = 128
ROPE_THETA = 10000.0
ROPE_HALF = HEAD_DIM // 2
NORM_EPS = 1e-6
LN_EPS = 1e-5
DN_ALPHA = (2.0 * DEPTH) ** 0.25
DN_BETA = (8.0 * DEPTH) ** -0.25
SPLIT_SIZES = (ATTN_WIDTH, KV_WIDTH, KV_WIDTH, ATTN_WIDTH,
               CONV_WIDTH, CONV_WIDTH, CONV_WIDTH, CONV_WIDTH,
               D_MODEL, D_MODEL)
SPLIT_POINTS = tuple(int(v) for v in np.cumsum(SPLIT_SIZES)[:-1])
IN_WIDTH = int(sum(SPLIT_SIZES))

kernel_name = "hybrid_gqa_axialrope_shortconv_gated_deepnorm"


def _rms_norm(x, g):
    xf = x.astype(jnp.float32)
    xf = xf * lax.rsqrt(jnp.mean(xf * xf, axis=-1, keepdims=True) + NORM_EPS)
    return xf.astype(x.dtype) * g


def _layer_norm(x, g, b):
    xf = x.astype(jnp.float32)
    mu = jnp.mean(xf, axis=-1, keepdims=True)
    var = jnp.mean(jnp.square(xf - mu), axis=-1, keepdims=True)
    return ((xf - mu) * lax.rsqrt(var + LN_EPS)).astype(x.dtype) * g + b


def _rotate(u, cos, sin):
    m = u.shape[-1] // 2
    u1, u2 = u[..., :m], u[..., m:]
    return jnp.concatenate([u1 * cos - u2 * sin, u2 * cos + u1 * sin], axis=-1)


def _axial_tables(seq_len, dtype):
    rows = seq_len // GRID_W
    row = jnp.repeat(jnp.arange(rows, dtype=jnp.float32), GRID_W)
    col = jnp.tile(jnp.arange(GRID_W, dtype=jnp.float32), rows)
    inv_freq = ROPE_THETA ** (-jnp.arange(0, ROPE_HALF, 2, dtype=jnp.float32) / ROPE_HALF)
    ang_r = row[:, None, None] * inv_freq
    ang_c = col[:, None, None] * inv_freq
    return (jnp.cos(ang_r).astype(dtype), jnp.sin(ang_r).astype(dtype),
            jnp.cos(ang_c).astype(dtype), jnp.sin(ang_c).astype(dtype))


def _axial_rope(x, tabs):
    cr, sr, cc, sc = tabs
    return jnp.concatenate([_rotate(x[..., :ROPE_HALF], cr, sr),
                            _rotate(x[..., ROPE_HALF:], cc, sc)], axis=-1)


def _gqa_attention(q, k, v):
    B, S = q.shape[0], q.shape[1]
    nblk = S // Q_BLOCK
    scale = 1.0 / math.sqrt(HEAD_DIM)
    qb = q.reshape(B, nblk, Q_BLOCK, N_KV_HEADS, GQA_GROUP, HEAD_DIM).transpose(1, 0, 2, 3, 4, 5)

    def block(qi):
        s = jnp.einsum('bqkgd,bskd->bkgqs', qi, k).astype(jnp.float32) * scale
        p = jax.nn.softmax(s, axis=-1).astype(v.dtype)
        return jnp.einsum('bkgqs,bskd->bqkgd', p, v)

    o = lax.map(block, qb)
    return o.transpose(1, 0, 2, 3, 4, 5).reshape(B, S, ATTN_WIDTH)


def _short_conv(u, w, b):
    up = jnp.pad(u, ((0, 0), (1, 1), (0, 0)))
    return up[:, :-2] * w[0] + up[:, 1:-1] * w[1] + up[:, 2:] * w[2] + b


def _layer(x, w_in, b_in, q_gain, k_gain, conv_w, conv_b,
           w_attn_out, w_conv_out, w_o, ln_g, ln_b):
    B, S, _ = x.shape
    proj = jnp.einsum('bsd,df->bsf', x, w_in) + b_in
    q, k, v, z_a, gb, gc, h, z_c, g_a, g_c = jnp.split(proj, SPLIT_POINTS, axis=-1)

    tabs = _axial_tables(S, x.dtype)
    q = _axial_rope(_rms_norm(q.reshape(B, S, N_HEADS, HEAD_DIM), q_gain), tabs)
    k = _axial_rope(_rms_norm(k.reshape(B, S, N_KV_HEADS, HEAD_DIM), k_gain), tabs)
    v = v.reshape(B, S, N_KV_HEADS, HEAD_DIM)
    attn = _gqa_attention(q, k, v) * jax.nn.silu(z_a)
    a_out = jnp.einsum('bsf,fd->bsd', attn, w_attn_out)

    conv = gb * _short_conv(gc * h, conv_w, conv_b) * jax.nn.silu(z_c)
    c_out = jnp.einsum('bsf,fd->bsd', conv, w_conv_out)

    merged = jax.nn.sigmoid(g_a) * a_out + jax.nn.sigmoid(g_c) * c_out
    out = jnp.einsum('bsd,de->bse', merged, w_o)
    return _layer_norm(DN_ALPHA * x + out, ln_g, ln_b)


def setup_inputs(seed: int = 0) -> dict:
    key = jax.random.key(seed)
    ks = jax.random.split(key, 14)
    f32 = jnp.float32
    nrm = lambda k, shp: jax.random.normal(k, shp, dtype=f32)
    return {
        "x_prompt": nrm(ks[0], (BATCH, SEQ, D_MODEL)),
        "x_sample": nrm(ks[1], (DEC_BATCH, DEC_SEQ, D_MODEL)),
        "w_in": nrm(ks[2], (DEPTH, D_MODEL, IN_WIDTH)) * D_MODEL ** -0.5,
        "b_in": nrm(ks[3], (DEPTH, IN_WIDTH)) * 0.01,
        "q_gain": 1.0 + 0.02 * nrm(ks[4], (DEPTH, HEAD_DIM)),
        "k_gain": 1.0 + 0.02 * nrm(ks[5], (DEPTH, HEAD_DIM)),
        "conv_w": nrm(ks[6], (DEPTH, CONV_K, CONV_WIDTH)) * CONV_K ** -0.5,
        "conv_b": nrm(ks[7], (DEPTH, CONV_WIDTH)) * 0.01,
        "w_attn_out": nrm(ks[8], (DEPTH, ATTN_WIDTH, D_MODEL)) * ATTN_WIDTH ** -0.5 * DN_BETA,
        "w_conv_out": nrm(ks[9], (DEPTH, CONV_WIDTH, D_MODEL)) * CONV_WIDTH ** -0.5 * DN_BETA,
        "w_o": nrm(ks[10], (DEPTH, D_MODEL, D_MODEL)) * D_MODEL ** -0.5 * DN_BETA,
        "ln_g": 1.0 + 0.02 * nrm(ks[11], (DEPTH, D_MODEL)),
        "ln_b": 0.01 * nrm(ks[12], (DEPTH, D_MODEL)),
    }


def reference(x_prompt, x_sample, w_in, b_in, q_gain, k_gain, conv_w, conv_b,
              w_attn_out, w_conv_out, w_o, ln_g, ln_b):
    y_prompt = x_prompt
    y_sample = x_sample
    for l in range(DEPTH):
        p = (w_in[l], b_in[l], q_gain[l], k_gain[l], conv_w[l], conv_b[l],
             w_attn_out[l], w_conv_out[l], w_o[l], ln_g[l], ln_b[l])
        y_prompt = _layer(y_prompt, *p)
        y_sample = _layer(y_sample, *p)
    return (y_prompt, y_sample)
```

```python
import functools
import math

import jax
import jax.numpy as jnp
import numpy as np
from jax import lax
from jax.experimental import pallas as pl
from jax.experimental.pallas import tpu as pltpu

D_MODEL = 1024
HEAD_DIM = 64
N_HEADS = 16
N_KV_HEADS = 4
GQA_GROUP = N_HEADS // N_KV_HEADS
KV_WIDTH = N_KV_HEADS * HEAD_DIM
QKV_WIDTH = D_MODEL + 2 * KV_WIDTH
GRID_W = 64
ROPE_THETA = 10000.0
ROPE_HALF = HEAD_DIM // 2
ROPE_PAIR = ROPE_HALF // 2
NORM_EPS = 1e-6
LN_EPS = 1e-5
DN_ALPHA = 2.0 ** 0.25
SCORE_SCALE = 1.0 / math.sqrt(HEAD_DIM)

LANES = 128
SUBLANES = 8
BF16_ROWS = 16
VMEM_LIMIT_BYTES = 56 * 1024 * 1024

K_PAD = LANES
V_ROWS = HEAD_DIM + BF16_ROWS
NEG_BIG = -1e30

TOKEN_TILE = 512
HALO = SUBLANES
Q_TILE = 256
KV_TILE = 512


def _params(sem):
    return pltpu.CompilerParams(dimension_semantics=sem,
                                vmem_limit_bytes=VMEM_LIMIT_BYTES)


def _full(shape):
    return pl.BlockSpec(shape, lambda *_: (0,) * len(shape))


def _norm_rope_t(u, gain, cr, sr, cc, sc):
    ms = jnp.mean(u * u, axis=0, keepdims=True)
    u = u * lax.rsqrt(ms + NORM_EPS) * gain
    p = ROPE_PAIR
    r1, r2, c1, c2 = u[0:p], u[p:2 * p], u[2 * p:3 * p], u[3 * p:4 * p]
    return jnp.concatenate([r1 * cr - r2 * sr, r2 * cr + r1 * sr,
                            c1 * cc - c2 * sc, c2 * cc + c1 * sc], axis=0)


def _qkv_kernel(x_ref, w_ref, b_ref, qg_ref, kg_ref, cr_ref, sr_ref, cc_ref, sc_ref,
                qt_ref, k_ref, vt_ref, *, kv_tile):
    tm = x_ref.shape[0]
    xb = x_ref[...].astype(jnp.bfloat16)
    pt = lax.dot_general(w_ref[...], xb, (((1,), (1,)), ((), ())),
                         preferred_element_type=jnp.float32) + b_ref[...]
    cr, sr, cc, sc = cr_ref[...], sr_ref[...], cc_ref[...], sc_ref[...]
    qg, kg = qg_ref[...], kg_ref[...]
    for h in range(N_HEADS):
        u = pt[h * HEAD_DIM:(h + 1) * HEAD_DIM]
        qt_ref[h * HEAD_DIM:(h + 1) * HEAD_DIM, :] = (
            _norm_rope_t(u, qg, cr, sr, cc, sc).astype(jnp.bfloat16))
    zeros = jnp.zeros((K_PAD - HEAD_DIM, tm), jnp.float32)
    kparts = []
    for g in range(N_KV_HEADS):
        u = pt[D_MODEL + g * HEAD_DIM:D_MODEL + (g + 1) * HEAD_DIM]
        kparts += [_norm_rope_t(u, kg, cr, sr, cc, sc), zeros]
    kt = jnp.concatenate(kparts, axis=0)
    k_ref[...] = kt.T.astype(jnp.bfloat16)
    ones = jnp.ones((BF16_ROWS, kv_tile), jnp.bfloat16)
    for g in range(N_KV_HEADS):
        v = pt[D_MODEL + KV_WIDTH + g * HEAD_DIM:D_MODEL + KV_WIDTH + (g + 1) * HEAD_DIM]
        v = v.astype(jnp.bfloat16)
        for c in range(tm // kv_tile):
            vt_ref[g, c, 0:HEAD_DIM, :] = v[:, c * kv_tile:(c + 1) * kv_tile]
            vt_ref[g, c, HEAD_DIM:V_ROWS, :] = ones


def _qkv_call(x2, w_qkv_t, b_qkv, qg, kg, tabs, *, seq_len, tm, kv_tile):
    n_tok = x2.shape[0]
    tiles_per_seq = seq_len // tm
    tab_spec = pl.BlockSpec((ROPE_PAIR, tm), lambda i: (0, i % tiles_per_seq))
    return pl.pallas_call(
        functools.partial(_qkv_kernel, kv_tile=kv_tile),
        grid=(n_tok // tm,),
        in_specs=[
            pl.BlockSpec((tm, D_MODEL), lambda i: (i, 0)),
            _full((QKV_WIDTH, D_MODEL)),
            _full((QKV_WIDTH, 1)),
            _full((HEAD_DIM, 1)),
            _full((HEAD_DIM, 1)),
            tab_spec, tab_spec, tab_spec, tab_spec,
        ],
        out_specs=[
            pl.BlockSpec((D_MODEL, tm), lambda i: (0, i)),
            pl.BlockSpec((tm, N_KV_HEADS * K_PAD), lambda i: (i, 0)),
            pl.BlockSpec((N_KV_HEADS, tm // kv_tile, V_ROWS, kv_tile),
                         lambda i: (0, i, 0, 0)),
        ],
        out_shape=[
            jax.ShapeDtypeStruct((D_MODEL, n_tok), jnp.bfloat16),
            jax.ShapeDtypeStruct((n_tok, N_KV_HEADS * K_PAD), jnp.bfloat16),
            jax.ShapeDtypeStruct((N_KV_HEADS, n_tok // kv_tile, V_ROWS, kv_tile),
                                 jnp.bfloat16),
        ],
        compiler_params=_params(("parallel",)),
        name="qkv",
    )(x2, w_qkv_t, b_qkv, qg, kg, *tabs)


def _silu(x):
    return x * jax.nn.sigmoid(x)


def _gates_kernel(x_ref, xp_ref, xn_ref, w_ref, b_ref, cw_ref, cb_ref, wco_ref,
                  sz_ref, sga_ref, cg_ref, *, tiles_per_seq):
    tm = x_ref.shape[0]
    i = pl.program_id(0)
    x = x_ref[...]
    xb = x.astype(jnp.bfloat16)
    xcat = jnp.concatenate([xp_ref[...], x, xn_ref[...]], axis=0).astype(jnp.bfloat16)

    def proj(lhs, c):
        sl = slice(c * D_MODEL, (c + 1) * D_MODEL)
        return jnp.dot(lhs, w_ref[:, sl], preferred_element_type=jnp.float32) + b_ref[:, sl]

    sz_ref[...] = _silu(proj(xb, 0))
    sga_ref[...] = jax.nn.sigmoid(proj(xb, 5))

    u = proj(xcat, 2) * proj(xcat, 3)
    rows = lax.broadcasted_iota(jnp.int32, u.shape, 0)
    first = (i % tiles_per_seq) == 0
    last = (i % tiles_per_seq) == tiles_per_seq - 1
    outside = ((rows < HALO) & first) | ((rows >= tm + HALO) & last)
    u = jnp.where(outside, 0.0, u)
    n = tm + 2 * HALO
    u_prev = pltpu.roll(u, 1, axis=0)[HALO:HALO + tm]
    u_next = pltpu.roll(u, n - 1, axis=0)[HALO:HALO + tm]
    cw = cw_ref[...]
    conv = (u_prev * cw[0:1] + u[HALO:HALO + tm] * cw[1:2] + u_next * cw[2:3]
            + cb_ref[...])
    act = proj(xb, 1) * conv * _silu(proj(xb, 4))
    c_out = jnp.dot(act.astype(jnp.bfloat16), wco_ref[...],
                    preferred_element_type=jnp.float32)
    cg_ref[...] = jax.nn.sigmoid(proj(xb, 6)) * c_out


def _gates_call(x2, w_rest, b_rest, conv_w, conv_b, w_conv_out, *, seq_len, tm):
    n_tok = x2.shape[0]
    halo_blocks = tm // HALO
    last_halo = n_tok // HALO - 1
    row_spec = pl.BlockSpec((tm, D_MODEL), lambda i: (i, 0))
    out = jax.ShapeDtypeStruct((n_tok, D_MODEL), jnp.float32)
    return pl.pallas_call(
        functools.partial(_gates_kernel, tiles_per_seq=seq_len // tm),
        grid=(n_tok // tm,),
        in_specs=[
            row_spec,
            pl.BlockSpec((HALO, D_MODEL),
                         lambda i: (jnp.maximum(i * halo_blocks - 1, 0), 0)),
            pl.BlockSpec((HALO, D_MODEL),
                         lambda i: (jnp.minimum((i + 1) * halo_blocks, last_halo), 0)),
            _full(w_rest.shape),
            _full(b_rest.shape),
            _full(conv_w.shape),
            _full(conv_b.shape),
            _full(w_conv_out.shape),
        ],
        out_specs=[row_spec, row_spec, row_spec],
        out_shape=[out, out, out],
        compiler_params=_params(("parallel",)),
        name="gates",
    )(x2, x2, x2, w_rest, b_rest, conv_w, conv_b, w_conv_out)


def _attn_kernel(qt_ref, k_ref, vt_ref, o_ref, qz_ref, m_ref, acc_ref, *, kv_tile):
    tq = qt_ref.shape[1]
    n_kv = k_ref.shape[0] // kv_tile
    zeros = jnp.zeros((K_PAD - HEAD_DIM, tq), jnp.bfloat16)
    for h in range(GQA_GROUP):
        qz_ref[0:HEAD_DIM, h * tq:(h + 1) * tq] = qt_ref[h * HEAD_DIM:(h + 1) * HEAD_DIM, :]
        qz_ref[HEAD_DIM:K_PAD, h * tq:(h + 1) * tq] = zeros
    m_ref[...] = jnp.full(m_ref.shape, NEG_BIG, jnp.float32)
    acc_ref[...] = jnp.zeros(acc_ref.shape, jnp.float32)

    def body(j, carry):
        start = pl.multiple_of(j * kv_tile, kv_tile)
        s = jnp.dot(k_ref[pl.ds(start, kv_tile), :], qz_ref[...],
                    preferred_element_type=jnp.float32)
        m_prev = m_ref[...]
        m_new = jnp.maximum(m_prev, jnp.max(s, axis=0, keepdims=True))
        p = jnp.exp(s - m_new).astype(jnp.bfloat16)
        pv = jnp.dot(vt_ref[j], p, preferred_element_type=jnp.float32)
        acc_ref[...] = jnp.exp(m_prev - m_new) * acc_ref[...] + pv
        m_ref[...] = m_new
        return carry

    lax.fori_loop(0, n_kv, body, 0)

    acc = acc_ref[...]
    o = acc[0:HEAD_DIM] / acc[HEAD_DIM:HEAD_DIM + 1]
    ot = jnp.concatenate([o[:, h * tq:(h + 1) * tq] for h in range(GQA_GROUP)], axis=0)
    o_ref[...] = ot.T


def _attn_call(qt, k, vt, *, batch, seq_len, tq, kv_tile):
    n_tok = batch * seq_len
    nq = seq_len // tq
    kv_blocks = seq_len // kv_tile
    width = GQA_GROUP * tq
    return pl.pallas_call(
        functools.partial(_attn_kernel, kv_tile=kv_tile),
        grid=(batch, N_KV_HEADS, nq),
        in_specs=[
            pl.BlockSpec((GQA_GROUP * HEAD_DIM, tq), lambda b, g, i: (g, b * nq + i)),
            pl.BlockSpec((seq_len, K_PAD), lambda b, g, i: (b, g)),
            pl.BlockSpec((None, kv_blocks, V_ROWS, kv_tile), lambda b, g, i: (g, b, 0, 0)),
        ],
        out_specs=pl.BlockSpec((tq, GQA_GROUP * HEAD_DIM), lambda b, g, i: (b * nq + i, g)),
        out_shape=jax.ShapeDtypeStruct((n_tok, D_MODEL), jnp.float32),
        scratch_shapes=[
            pltpu.VMEM((K_PAD, width), jnp.bfloat16),
            pltpu.VMEM((1, width), jnp.float32),
            pltpu.VMEM((V_ROWS, width), jnp.float32),
        ],
        compiler_params=_params(("parallel", "parallel", "arbitrary")),
        name="attn",
    )(qt, k, vt)


def _out_kernel(x_ref, attn_ref, sz_ref, sga_ref, cg_ref, wao_ref, wo_ref, g_ref, b_ref,
                y_ref):
    a_in = (attn_ref[...] * sz_ref[...]).astype(jnp.bfloat16)
    a_out = jnp.dot(a_in, wao_ref[...], preferred_element_type=jnp.float32)
    merged = sga_ref[...] * a_out + cg_ref[...]
    out = jnp.dot(merged.astype(jnp.bfloat16), wo_ref[...],
                  preferred_element_type=jnp.float32)
    r = DN_ALPHA * x_ref[...] + out
    mu = jnp.mean(r, axis=-1, keepdims=True)
    d = r - mu
    var = jnp.mean(d * d, axis=-1, keepdims=True)
    y_ref[...] = d * lax.rsqrt(var + LN_EPS) * g_ref[...] + b_ref[...]


def _out_call(x2, attn, sz, sga, cg, w_attn_out, w_o, ln_g, ln_b, *, tm):
    n_tok = x2.shape[0]
    row_spec = pl.BlockSpec((tm, D_MODEL), lambda i: (i, 0))
    return pl.pallas_call(
        _out_kernel,
        grid=(n_tok // tm,),
        in_specs=[row_spec] * 5 + [_full(w_attn_out.shape), _full(w_o.shape),
                                    _full(ln_g.shape), _full(ln_b.shape)],
        out_specs=row_spec,
        out_shape=jax.ShapeDtypeStruct((n_tok, D_MODEL), jnp.float32),
        compiler_params=_params(("parallel",)),
        name="out",
    )(x2, attn, sz, sga, cg, w_attn_out, w_o, ln_g, ln_b)


def _axial_tables_t(seq_len):
    rows = seq_len // GRID_W
    row = jnp.repeat(jnp.arange(rows, dtype=jnp.float32), GRID_W)
    col = jnp.tile(jnp.arange(GRID_W, dtype=jnp.float32), rows)
    inv_freq = ROPE_THETA ** (-jnp.arange(0, ROPE_HALF, 2, dtype=jnp.float32) / ROPE_HALF)
    ang_r = row[:, None] * inv_freq
    ang_c = col[:, None] * inv_freq
    return tuple(t.T for t in (jnp.cos(ang_r), jnp.sin(ang_r), jnp.cos(ang_c), jnp.sin(ang_c)))


def _tile(n, want):
    t = min(n, want)
    assert n % t == 0, (n, t)
    return t


def _layer(x, p):
    batch, seq_len, _ = x.shape
    x2 = x.reshape(batch * seq_len, D_MODEL)
    tm = _tile(seq_len, TOKEN_TILE)
    tq = _tile(seq_len, Q_TILE)
    kv_tile = _tile(tm, KV_TILE)
    tabs = _axial_tables_t(seq_len)
    qt, k, vt = _qkv_call(x2, p["w_qkv_t"], p["b_qkv"], p["q_gain"], p["k_gain"], tabs,
                          seq_len=seq_len, tm=tm, kv_tile=kv_tile)
    sz, sga, cg = _gates_call(x2, p["w_rest"], p["b_rest"], p["conv_w"], p["conv_b"],
                              p["w_conv_out"], seq_len=seq_len, tm=tm)
    attn = _attn_call(qt, k, vt, batch=batch, seq_len=seq_len, tq=tq, kv_tile=kv_tile)
    y = _out_call(x2, attn, sz, sga, cg, p["w_attn_out"], p["w_o"], p["ln_g"], p["ln_b"],
                  tm=tm)
    return y.reshape(batch, seq_len, D_MODEL)


def _prepare(w_in, b_in, q_gain, k_gain, conv_w, conv_b, w_attn_out, w_conv_out, w_o,
             ln_g, ln_b):
    bf = jnp.bfloat16
    return {
        "w_qkv_t": w_in[:, :QKV_WIDTH].T.astype(bf),
        "b_qkv": b_in[:QKV_WIDTH].reshape(QKV_WIDTH, 1),
        "q_gain": (q_gain * SCORE_SCALE).reshape(HEAD_DIM, 1),
        "k_gain": k_gain.reshape(HEAD_DIM, 1),
        "w_rest": w_in[:, QKV_WIDTH:].astype(bf),
        "b_rest": b_in[QKV_WIDTH:].reshape(1, -1),
        "conv_w": conv_w,
        "conv_b": conv_b.reshape(1, D_MODEL),
        "w_conv_out": w_conv_out.astype(bf),
        "w_attn_out": w_attn_out.astype(bf),
        "w_o": w_o.astype(bf),
        "ln_g": ln_g.reshape(1, D_MODEL),
        "ln_b": ln_b.reshape(1, D_MODEL),
    }


def kernel(x_prompt, x_sample, w_in, b_in, q_gain, k_gain, conv_w, conv_b, w_attn_out,
           w_conv_out, w_o, ln_g, ln_b):
    depth = w_in.shape[0]
    y_prompt, y_sample = x_prompt, x_sample
    for l in range(depth):
        p = _prepare(w_in[l], b_in[l], q_gain[l], k_gain[l], conv_w[l], conv_b[l],
                     w_attn_out[l], w_conv_out[l], w_o[l], ln_g[l], ln_b[l])
        y_prompt = _layer(y_prompt, p)
        y_sample = _layer(y_sample, p)
    return (y_prompt, y_sample)
```

```python
import functools
import math

import jax
import jax.numpy as jnp
import numpy as np
from jax import lax
from jax.experimental import pallas as pl
from jax.experimental.pallas import tpu as pltpu

D_MODEL = 1024
HEAD_DIM = 64
N_HEADS = 16
N_KV_HEADS = 4
GQA_GROUP = N_HEADS // N_KV_HEADS
KV_WIDTH = N_KV_HEADS * HEAD_DIM
QKV_WIDTH = D_MODEL + 2 * KV_WIDTH
GRID_W = 64
ROPE_THETA = 10000.0
ROPE_HALF = HEAD_DIM // 2
ROPE_PAIR = ROPE_HALF // 2
NORM_EPS = 1e-6
LN_EPS = 1e-5
DN_ALPHA = 2.0 ** 0.25
SCORE_SCALE = 1.0 / math.sqrt(HEAD_DIM)
LOG2_E = math.log2(math.e)

LANES = 128
SUBLANES = 8
BF16_ROWS = 16
VMEM_LIMIT_BYTES = 56 * 1024 * 1024

K_PAD = LANES
V_ROWS = HEAD_DIM + BF16_ROWS
NEG_BIG = -1e30
SAFE_LOGIT_BOUND = 30.0

TOKEN_TILE = 512
HALO = SUBLANES
Q_TILE = 256
KV_TILE = 512
KV_UNROLL = 8


def _params(sem):
    return pltpu.CompilerParams(dimension_semantics=sem,
                                vmem_limit_bytes=VMEM_LIMIT_BYTES)


def _full(shape):
    return pl.BlockSpec(shape, lambda *_: (0,) * len(shape))


def _norm_rope_t(u, gain, cr, sr, cc, sc):
    ms = jnp.mean(u * u, axis=0, keepdims=True)
    u = u * lax.rsqrt(ms + NORM_EPS) * gain
    p = ROPE_PAIR
    r1, r2, c1, c2 = u[0:p], u[p:2 * p], u[2 * p:3 * p], u[3 * p:4 * p]
    return jnp.concatenate([r1 * cr - r2 * sr, r2 * cr + r1 * sr,
                            c1 * cc - c2 * sc, c2 * cc + c1 * sc], axis=0)


def _qkv_kernel(x_ref, w_ref, b_ref, qg_ref, kg_ref, cr_ref, sr_ref, cc_ref, sc_ref,
                qt_ref, k_ref, vt_ref, *, kv_tile):
    tm = x_ref.shape[0]
    xb = x_ref[...].astype(jnp.bfloat16)
    pt = lax.dot_general(w_ref[...], xb, (((1,), (1,)), ((), ())),
                         preferred_element_type=jnp.float32) + b_ref[...]
    cr, sr, cc, sc = cr_ref[...], sr_ref[...], cc_ref[...], sc_ref[...]
    qg, kg = qg_ref[...], kg_ref[...]
    for h in range(N_HEADS):
        u = pt[h * HEAD_DIM:(h + 1) * HEAD_DIM]
        qt_ref[h * HEAD_DIM:(h + 1) * HEAD_DIM, :] = (
            _norm_rope_t(u, qg, cr, sr, cc, sc).astype(jnp.bfloat16))
    zeros = jnp.zeros((K_PAD - HEAD_DIM, tm), jnp.float32)
    kparts = []
    for g in range(N_KV_HEADS):
        u = pt[D_MODEL + g * HEAD_DIM:D_MODEL + (g + 1) * HEAD_DIM]
        kparts += [_norm_rope_t(u, kg, cr, sr, cc, sc), zeros]
    kt = jnp.concatenate(kparts, axis=0)
    k_ref[...] = kt.T.astype(jnp.bfloat16)
    ones = jnp.ones((BF16_ROWS, kv_tile), jnp.bfloat16)
    for g in range(N_KV_HEADS):
        v = pt[D_MODEL + KV_WIDTH + g * HEAD_DIM:D_MODEL + KV_WIDTH + (g + 1) * HEAD_DIM]
        v = v.astype(jnp.bfloat16)
        for c in range(tm // kv_tile):
            vt_ref[g, c, 0:HEAD_DIM, :] = v[:, c * kv_tile:(c + 1) * kv_tile]
            vt_ref[g, c, HEAD_DIM:V_ROWS, :] = ones


def _qkv_call(x2, w_qkv_t, b_qkv, qg, kg, tabs, *, seq_len, tm, kv_tile):
    n_tok = x2.shape[0]
    tiles_per_seq = seq_len // tm
    tab_spec = pl.BlockSpec((ROPE_PAIR, tm), lambda i: (0, i % tiles_per_seq))
    return pl.pallas_call(
        functools.partial(_qkv_kernel, kv_tile=kv_tile),
        grid=(n_tok // tm,),
        in_specs=[
            pl.BlockSpec((tm, D_MODEL), lambda i: (i, 0)),
            _full((QKV_WIDTH, D_MODEL)),
            _full((QKV_WIDTH, 1)),
            _full((HEAD_DIM, 1)),
            _full((HEAD_DIM, 1)),
            tab_spec, tab_spec, tab_spec, tab_spec,
        ],
        out_specs=[
            pl.BlockSpec((D_MODEL, tm), lambda i: (0, i)),
            pl.BlockSpec((tm, N_KV_HEADS * K_PAD), lambda i: (i, 0)),
            pl.BlockSpec((N_KV_HEADS, tm // kv_tile, V_ROWS, kv_tile),
                         lambda i: (0, i, 0, 0)),
        ],
        out_shape=[
            jax.ShapeDtypeStruct((D_MODEL, n_tok), jnp.bfloat16),
            jax.ShapeDtypeStruct((n_tok, N_KV_HEADS * K_PAD), jnp.bfloat16),
            jax.ShapeDtypeStruct((N_KV_HEADS, n_tok // kv_tile, V_ROWS, kv_tile),
                                 jnp.bfloat16),
        ],
        compiler_params=_params(("parallel",)),
        name="qkv",
    )(x2, w_qkv_t, b_qkv, qg, kg, *tabs)


def _silu(x):
    return x * jax.nn.sigmoid(x)


def _gates_kernel(x_ref, xp_ref, xn_ref, w_ref, b_ref, cw_ref, cb_ref, wco_ref,
                  sz_ref, sga_ref, cg_ref, *, tiles_per_seq):
    tm = x_ref.shape[0]
    i = pl.program_id(0)
    x = x_ref[...]
    xb = x.astype(jnp.bfloat16)
    xcat = jnp.concatenate([xp_ref[...], x, xn_ref[...]], axis=0).astype(jnp.bfloat16)

    def proj(lhs, c):
        sl = slice(c * D_MODEL, (c + 1) * D_MODEL)
        return jnp.dot(lhs, w_ref[:, sl], preferred_element_type=jnp.float32) + b_ref[:, sl]

    sz_ref[...] = _silu(proj(xb, 0))
    sga_ref[...] = jax.nn.sigmoid(proj(xb, 5))

    u = proj(xcat, 2) * proj(xcat, 3)
    rows = lax.broadcasted_iota(jnp.int32, u.shape, 0)
    first = (i % tiles_per_seq) == 0
    last = (i % tiles_per_seq) == tiles_per_seq - 1
    outside = ((rows < HALO) & first) | ((rows >= tm + HALO) & last)
    u = jnp.where(outside, 0.0, u)
    n = tm + 2 * HALO
    u_prev = pltpu.roll(u, 1, axis=0)[HALO:HALO + tm]
    u_next = pltpu.roll(u, n - 1, axis=0)[HALO:HALO + tm]
    cw = cw_ref[...]
    conv = (u_prev * cw[0:1] + u[HALO:HALO + tm] * cw[1:2] + u_next * cw[2:3]
            + cb_ref[...])
    act = proj(xb, 1) * conv * _silu(proj(xb, 4))
    c_out = jnp.dot(act.astype(jnp.bfloat16), wco_ref[...],
                    preferred_element_type=jnp.float32)
    cg_ref[...] = jax.nn.sigmoid(proj(xb, 6)) * c_out


def _gates_call(x2, w_rest, b_rest, conv_w, conv_b, w_conv_out, *, seq_len, tm):
    n_tok = x2.shape[0]
    halo_blocks = tm // HALO
    last_halo = n_tok // HALO - 1
    row_spec = pl.BlockSpec((tm, D_MODEL), lambda i: (i, 0))
    out = jax.ShapeDtypeStruct((n_tok, D_MODEL), jnp.float32)
    return pl.pallas_call(
        functools.partial(_gates_kernel, tiles_per_seq=seq_len // tm),
        grid=(n_tok // tm,),
        in_specs=[
            row_spec,
            pl.BlockSpec((HALO, D_MODEL),
                         lambda i: (jnp.maximum(i * halo_blocks - 1, 0), 0)),
            pl.BlockSpec((HALO, D_MODEL),
                         lambda i: (jnp.minimum((i + 1) * halo_blocks, last_halo), 0)),
            _full(w_rest.shape),
            _full(b_rest.shape),
            _full(conv_w.shape),
            _full(conv_b.shape),
            _full(w_conv_out.shape),
        ],
        out_specs=[row_spec, row_spec, row_spec],
        out_shape=[out, out, out],
        compiler_params=_params(("parallel",)),
        name="gates",
    )(x2, x2, x2, w_rest, b_rest, conv_w, conv_b, w_conv_out)


def _stage_queries(qt_ref, qz_ref):
    tq = qt_ref.shape[1]
    zeros = jnp.zeros((K_PAD - HEAD_DIM, tq), jnp.bfloat16)
    for h in range(GQA_GROUP):
        qz_ref[0:HEAD_DIM, h * tq:(h + 1) * tq] = qt_ref[h * HEAD_DIM:(h + 1) * HEAD_DIM, :]
        qz_ref[HEAD_DIM:K_PAD, h * tq:(h + 1) * tq] = zeros


def _store_attention(acc, o_ref):
    tq = o_ref.shape[0]
    o = acc[0:HEAD_DIM] / acc[HEAD_DIM:HEAD_DIM + 1]
    ot = jnp.concatenate([o[:, h * tq:(h + 1) * tq] for h in range(GQA_GROUP)], axis=0)
    o_ref[...] = ot.T


def _attn_bounded_kernel(qt_ref, k_ref, vt_ref, o_ref, qz_ref, acc_ref, *, kv_tile, unroll):
    n_kv = k_ref.shape[0] // kv_tile
    _stage_queries(qt_ref, qz_ref)
    acc_ref[...] = jnp.zeros(acc_ref.shape, jnp.float32)

    def probs(j):
        start = pl.multiple_of(j * kv_tile, kv_tile)
        s = jnp.dot(k_ref[pl.ds(start, kv_tile), :], qz_ref[...],
                    preferred_element_type=jnp.float32)
        return jnp.exp2(s).astype(jnp.bfloat16)

    def group(jj, carry):
        base = jj * unroll
        acc = acc_ref[...]
        p = probs(base)
        for t in range(unroll):
            if t + 1 < unroll:
                p_next = probs(base + t + 1)
            acc = acc + jnp.dot(vt_ref[base + t], p, preferred_element_type=jnp.float32)
            if t + 1 < unroll:
                p = p_next
        acc_ref[...] = acc
        return carry

    lax.fori_loop(0, n_kv // unroll, group, 0)
    _store_attention(acc_ref[...], o_ref)


def _attn_online_kernel(qt_ref, k_ref, vt_ref, o_ref, qz_ref, acc_ref, m_ref, *, kv_tile,
                        unroll):
    n_kv = k_ref.shape[0] // kv_tile
    _stage_queries(qt_ref, qz_ref)
    m_ref[...] = jnp.full(m_ref.shape, NEG_BIG, jnp.float32)
    acc_ref[...] = jnp.zeros(acc_ref.shape, jnp.float32)

    def scores(j):
        start = pl.multiple_of(j * kv_tile, kv_tile)
        s = jnp.dot(k_ref[pl.ds(start, kv_tile), :], qz_ref[...],
                    preferred_element_type=jnp.float32)
        return s, jnp.max(s, axis=0, keepdims=True)

    def group(jj, carry):
        base = jj * unroll
        s, s_max = scores(base)
        for t in range(unroll):
            if t + 1 < unroll:
                s_next, s_max_next = scores(base + t + 1)
            m_prev = m_ref[...]
            m_new = jnp.maximum(m_prev, s_max)
            p = jnp.exp2(s - m_new).astype(jnp.bfloat16)
            pv = jnp.dot(vt_ref[base + t], p, preferred_element_type=jnp.float32)
            acc_ref[...] = jnp.exp2(m_prev - m_new) * acc_ref[...] + pv
            m_ref[...] = m_new
            if t + 1 < unroll:
                s, s_max = s_next, s_max_next
        return carry

    lax.fori_loop(0, n_kv // unroll, group, 0)
    _store_attention(acc_ref[...], o_ref)


def _attn_call(qt, k, vt, *, bounded, batch, seq_len, tq, kv_tile):
    n_tok = batch * seq_len
    nq = seq_len // tq
    kv_blocks = seq_len // kv_tile
    width = GQA_GROUP * tq
    unroll = _tile(kv_blocks, KV_UNROLL)
    body = _attn_bounded_kernel if bounded else _attn_online_kernel
    scratch = [pltpu.VMEM((K_PAD, width), jnp.bfloat16),
               pltpu.VMEM((V_ROWS, width), jnp.float32)]
    if not bounded:
        scratch.append(pltpu.VMEM((1, width), jnp.float32))
    return pl.pallas_call(
        functools.partial(body, kv_tile=kv_tile, unroll=unroll),
        grid=(batch, N_KV_HEADS, nq),
        in_specs=[
            pl.BlockSpec((GQA_GROUP * HEAD_DIM, tq), lambda b, g, i: (g, b * nq + i)),
            pl.BlockSpec((seq_len, K_PAD), lambda b, g, i: (b, g)),
            pl.BlockSpec((None, kv_blocks, V_ROWS, kv_tile), lambda b, g, i: (g, b, 0, 0)),
        ],
        out_specs=pl.BlockSpec((tq, GQA_GROUP * HEAD_DIM), lambda b, g, i: (b * nq + i, g)),
        out_shape=jax.ShapeDtypeStruct((n_tok, D_MODEL), jnp.float32),
        scratch_shapes=scratch,
        compiler_params=_params(("parallel", "parallel", "arbitrary")),
        name="attn_bounded" if bounded else "attn_online",
    )(qt, k, vt)


def _out_kernel(x_ref, attn_ref, sz_ref, sga_ref, cg_ref, wao_ref, wo_ref, g_ref, b_ref,
                y_ref):
    a_in = (attn_ref[...] * sz_ref[...]).astype(jnp.bfloat16)
    a_out = jnp.dot(a_in, wao_ref[...], preferred_element_type=jnp.float32)
    merged = sga_ref[...] * a_out + cg_ref[...]
    out = jnp.dot(merged.astype(jnp.bfloat16), wo_ref[...],
                  preferred_element_type=jnp.float32)
    r = DN_ALPHA * x_ref[...] + out
    mu = jnp.mean(r, axis=-1, keepdims=True)
    d = r - mu
    var = jnp.mean(d * d, axis=-1, keepdims=True)
    y_ref[...] = d * lax.rsqrt(var + LN_EPS) * g_ref[...] + b_ref[...]


def _out_call(x2, attn, sz, sga, cg, w_attn_out, w_o, ln_g, ln_b, *, tm):
    n_tok = x2.shape[0]
    row_spec = pl.BlockSpec((tm, D_MODEL), lambda i: (i, 0))
    return pl.pallas_call(
        _out_kernel,
        grid=(n_tok // tm,),
        in_specs=[row_spec] * 5 + [_full(w_attn_out.shape), _full(w_o.shape),
                                    _full(ln_g.shape), _full(ln_b.shape)],
        out_specs=row_spec,
        out_shape=jax.ShapeDtypeStruct((n_tok, D_MODEL), jnp.float32),
        compiler_params=_params(("parallel",)),
        name="out",
    )(x2, attn, sz, sga, cg, w_attn_out, w_o, ln_g, ln_b)


def _axial_tables_t(seq_len):
    rows = seq_len // GRID_W
    row = jnp.repeat(jnp.arange(rows, dtype=jnp.float32), GRID_W)
    col = jnp.tile(jnp.arange(GRID_W, dtype=jnp.float32), rows)
    inv_freq = ROPE_THETA ** (-jnp.arange(0, ROPE_HALF, 2, dtype=jnp.float32) / ROPE_HALF)
    ang_r = row[:, None] * inv_freq
    ang_c = col[:, None] * inv_freq
    return tuple(t.T for t in (jnp.cos(ang_r), jnp.sin(ang_r), jnp.cos(ang_c), jnp.sin(ang_c)))


def _tile(n, want):
    t = min(n, want)
    assert n % t == 0, (n, t)
    return t


def _layer(x, p):
    batch, seq_len, _ = x.shape
    x2 = x.reshape(batch * seq_len, D_MODEL)
    tm = _tile(seq_len, TOKEN_TILE)
    tq = _tile(seq_len, Q_TILE)
    kv_tile = _tile(tm, KV_TILE)
    tabs = _axial_tables_t(seq_len)
    qt, k, vt = _qkv_call(x2, p["w_qkv_t"], p["b_qkv"], p["q_gain"], p["k_gain"], tabs,
                          seq_len=seq_len, tm=tm, kv_tile=kv_tile)
    sz, sga, cg = _gates_call(x2, p["w_rest"], p["b_rest"], p["conv_w"], p["conv_b"],
                              p["w_conv_out"], seq_len=seq_len, tm=tm)
    attend = functools.partial(_attn_call, batch=batch, seq_len=seq_len, tq=tq,
                               kv_tile=kv_tile)
    attn = lax.cond(p["logit_bound"] <= SAFE_LOGIT_BOUND,
                    functools.partial(attend, bounded=True),
                    functools.partial(attend, bounded=False), qt, k, vt)
    y = _out_call(x2, attn, sz, sga, cg, p["w_attn_out"], p["w_o"], p["ln_g"], p["ln_b"],
                  tm=tm)
    return y.reshape(batch, seq_len, D_MODEL)


def _prepare(w_in, b_in, q_gain, k_gain, conv_w, conv_b, w_attn_out, w_conv_out, w_o,
             ln_g, ln_b):
    bf = jnp.bfloat16
    return {
        "w_qkv_t": w_in[:, :QKV_WIDTH].T.astype(bf),
        "b_qkv": b_in[:QKV_WIDTH].reshape(QKV_WIDTH, 1),
        "q_gain": (q_gain * (SCORE_SCALE * LOG2_E)).reshape(HEAD_DIM, 1),
        "k_gain": k_gain.reshape(HEAD_DIM, 1),
        "logit_bound": (math.sqrt(HEAD_DIM) * jnp.max(jnp.abs(q_gain))
                        * jnp.max(jnp.abs(k_gain))),
        "w_rest": w_in[:, QKV_WIDTH:].astype(bf),
        "b_rest": b_in[QKV_WIDTH:].reshape(1, -1),
        "conv_w": conv_w,
        "conv_b": conv_b.reshape(1, D_MODEL),
        "w_conv_out": w_conv_out.astype(bf),
        "w_attn_out": w_attn_out.astype(bf),
        "w_o": w_o.astype(bf),
        "ln_g": ln_g.reshape(1, D_MODEL),
        "ln_b": ln_b.reshape(1, D_MODEL),
    }


def kernel(x_prompt, x_sample, w_in, b_in, q_gain, k_gain, conv_w, conv_b, w_attn_out,
           w_conv_out, w_o, ln_g, ln_b):
    depth = w_in.shape[0]
    y_prompt, y_sample = x_prompt, x_sample
    for l in range(depth):
        p = _prepare(w_in[l], b_in[l], q_gain[l], k_gain[l], conv_w[l], conv_b[l],
                     w_attn_out[l], w_conv_out[l], w_o[l], ln_g[l], ln_b[l])
        y_prompt = _layer(y_prompt, p)
        y_sample = _layer(y_sample, p)
    return (y_prompt, y_sample)
```

```python
import functools
import math

import jax
import jax.numpy as jnp
import numpy as np
from jax import lax
from jax.experimental import pallas as pl
from jax.experimental.pallas import tpu as pltpu

D_MODEL = 1024
HEAD_DIM = 64
N_HEADS = 16
N_KV_HEADS = 4
GQA_GROUP = N_HEADS // N_KV_HEADS
KV_WIDTH = N_KV_HEADS * HEAD_DIM
QKV_WIDTH = D_MODEL + 2 * KV_WIDTH
GRID_W = 64
ROPE_THETA = 10000.0
ROPE_HALF = HEAD_DIM // 2
ROPE_PAIR = ROPE_HALF // 2
NORM_EPS = 1e-6
LN_EPS = 1e-5
DN_ALPHA = 2.0 ** 0.25
SCORE_SCALE = 1.0 / math.sqrt(HEAD_DIM)
LOG2_E = math.log2(math.e)

LANES = 128
SUBLANES = 8
BF16_ROWS = 16
VMEM_LIMIT_BYTES = 56 * 1024 * 1024

K_PAD = LANES
NEG_BIG = -1e30
SAFE_LOGIT_BOUND = 30.0

TOKEN_TILE = 512
HALO = SUBLANES
Q_TILE = 256
KV_TILE = 512
COLUMN_STRIPS = 4
KV_UNROLL = 8


def _params(sem):
    return pltpu.CompilerParams(dimension_semantics=sem,
                                vmem_limit_bytes=VMEM_LIMIT_BYTES)


def _full(shape):
    return pl.BlockSpec(shape, lambda *_: (0,) * len(shape))


def _norm_rope_t(u, gain, cr, sr, cc, sc):
    ms = jnp.mean(u * u, axis=0, keepdims=True)
    u = u * lax.rsqrt(ms + NORM_EPS) * gain
    p = ROPE_PAIR
    r1, r2, c1, c2 = u[0:p], u[p:2 * p], u[2 * p:3 * p], u[3 * p:4 * p]
    return jnp.concatenate([r1 * cr - r2 * sr, r2 * cr + r1 * sr,
                            c1 * cc - c2 * sc, c2 * cc + c1 * sc], axis=0)


def _qkv_kernel(x_ref, w_ref, b_ref, qg_ref, kg_ref, cr_ref, sr_ref, cc_ref, sc_ref,
                qt_ref, k_ref, vt_ref, *, kv_tile):
    tm = x_ref.shape[0]
    xb = x_ref[...].astype(jnp.bfloat16)
    pt = lax.dot_general(w_ref[...], xb, (((1,), (1,)), ((), ())),
                         preferred_element_type=jnp.float32) + b_ref[...]
    cr, sr, cc, sc = cr_ref[...], sr_ref[...], cc_ref[...], sc_ref[...]
    qg, kg = qg_ref[...], kg_ref[...]
    for h in range(N_HEADS):
        u = pt[h * HEAD_DIM:(h + 1) * HEAD_DIM]
        qt_ref[h * HEAD_DIM:(h + 1) * HEAD_DIM, :] = (
            _norm_rope_t(u, qg, cr, sr, cc, sc).astype(jnp.bfloat16))
    zeros = jnp.zeros((K_PAD - HEAD_DIM, tm), jnp.float32)
    kparts = []
    for g in range(N_KV_HEADS):
        u = pt[D_MODEL + g * HEAD_DIM:D_MODEL + (g + 1) * HEAD_DIM]
        kparts += [_norm_rope_t(u, kg, cr, sr, cc, sc), zeros]
    kt = jnp.concatenate(kparts, axis=0)
    k_ref[...] = kt.T.astype(jnp.bfloat16)
    for g in range(N_KV_HEADS):
        v = pt[D_MODEL + KV_WIDTH + g * HEAD_DIM:D_MODEL + KV_WIDTH + (g + 1) * HEAD_DIM]
        v = v.astype(jnp.bfloat16)
        for c in range(tm // kv_tile):
            vt_ref[g, c] = v[:, c * kv_tile:(c + 1) * kv_tile]


def _qkv_call(x2, w_qkv_t, b_qkv, qg, kg, tabs, *, seq_len, tm, kv_tile):
    n_tok = x2.shape[0]
    tiles_per_seq = seq_len // tm
    tab_spec = pl.BlockSpec((ROPE_PAIR, tm), lambda i: (0, i % tiles_per_seq))
    return pl.pallas_call(
        functools.partial(_qkv_kernel, kv_tile=kv_tile),
        grid=(n_tok // tm,),
        in_specs=[
            pl.BlockSpec((tm, D_MODEL), lambda i: (i, 0)),
            _full((QKV_WIDTH, D_MODEL)),
            _full((QKV_WIDTH, 1)),
            _full((HEAD_DIM, 1)),
            _full((HEAD_DIM, 1)),
            tab_spec, tab_spec, tab_spec, tab_spec,
        ],
        out_specs=[
            pl.BlockSpec((D_MODEL, tm), lambda i: (0, i)),
            pl.BlockSpec((tm, N_KV_HEADS * K_PAD), lambda i: (i, 0)),
            pl.BlockSpec((N_KV_HEADS, tm // kv_tile, HEAD_DIM, kv_tile),
                         lambda i: (0, i, 0, 0)),
        ],
        out_shape=[
            jax.ShapeDtypeStruct((D_MODEL, n_tok), jnp.bfloat16),
            jax.ShapeDtypeStruct((n_tok, N_KV_HEADS * K_PAD), jnp.bfloat16),
            jax.ShapeDtypeStruct((N_KV_HEADS, n_tok // kv_tile, HEAD_DIM, kv_tile),
                                 jnp.bfloat16),
        ],
        compiler_params=_params(("parallel",)),
        name="qkv",
    )(x2, w_qkv_t, b_qkv, qg, kg, *tabs)


def _silu(x):
    return x * jax.nn.sigmoid(x)


def _gates_kernel(x_ref, xp_ref, xn_ref, w_ref, b_ref, cw_ref, cb_ref, wco_ref,
                  sz_ref, sga_ref, cg_ref, *, tiles_per_seq):
    tm = x_ref.shape[0]
    i = pl.program_id(0)
    x = x_ref[...]
    xb = x.astype(jnp.bfloat16)
    xcat = jnp.concatenate([xp_ref[...], x, xn_ref[...]], axis=0).astype(jnp.bfloat16)

    def proj(lhs, c):
        sl = slice(c * D_MODEL, (c + 1) * D_MODEL)
        return jnp.dot(lhs, w_ref[:, sl], preferred_element_type=jnp.float32) + b_ref[:, sl]

    sz_ref[...] = _silu(proj(xb, 0)).astype(sz_ref.dtype)
    sga_ref[...] = jax.nn.sigmoid(proj(xb, 5)).astype(sga_ref.dtype)

    u = proj(xcat, 2) * proj(xcat, 3)
    rows = lax.broadcasted_iota(jnp.int32, u.shape, 0)
    first = (i % tiles_per_seq) == 0
    last = (i % tiles_per_seq) == tiles_per_seq - 1
    outside = ((rows < HALO) & first) | ((rows >= tm + HALO) & last)
    u = jnp.where(outside, 0.0, u)
    n = tm + 2 * HALO
    u_prev = pltpu.roll(u, 1, axis=0)[HALO:HALO + tm]
    u_next = pltpu.roll(u, n - 1, axis=0)[HALO:HALO + tm]
    cw = cw_ref[...]
    conv = (u_prev * cw[0:1] + u[HALO:HALO + tm] * cw[1:2] + u_next * cw[2:3]
            + cb_ref[...])
    act = proj(xb, 1) * conv * _silu(proj(xb, 4))
    c_out = jnp.dot(act.astype(jnp.bfloat16), wco_ref[...],
                    preferred_element_type=jnp.float32)
    cg_ref[...] = jax.nn.sigmoid(proj(xb, 6)) * c_out


def _gates_call(x2, w_rest, b_rest, conv_w, conv_b, w_conv_out, *, seq_len, tm):
    n_tok = x2.shape[0]
    halo_blocks = tm // HALO
    last_halo = n_tok // HALO - 1
    row_spec = pl.BlockSpec((tm, D_MODEL), lambda i: (i, 0))
    gate = jax.ShapeDtypeStruct((n_tok, D_MODEL), jnp.bfloat16)
    branch = jax.ShapeDtypeStruct((n_tok, D_MODEL), jnp.float32)
    return pl.pallas_call(
        functools.partial(_gates_kernel, tiles_per_seq=seq_len // tm),
        grid=(n_tok // tm,),
        in_specs=[
            row_spec,
            pl.BlockSpec((HALO, D_MODEL),
                         lambda i: (jnp.maximum(i * halo_blocks - 1, 0), 0)),
            pl.BlockSpec((HALO, D_MODEL),
                         lambda i: (jnp.minimum((i + 1) * halo_blocks, last_halo), 0)),
            _full(w_rest.shape),
            _full(b_rest.shape),
            _full(conv_w.shape),
            _full(conv_b.shape),
            _full(w_conv_out.shape),
        ],
        out_specs=[row_spec, row_spec, row_spec],
        out_shape=[gate, gate, branch],
        compiler_params=_params(("parallel",)),
        name="gates",
    )(x2, x2, x2, w_rest, b_rest, conv_w, conv_b, w_conv_out)


def _stage_queries(qt_ref, qz_ref):
    tq = qt_ref.shape[1]
    zeros = jnp.zeros((K_PAD - HEAD_DIM, tq), jnp.bfloat16)
    for h in range(GQA_GROUP):
        qz_ref[0:HEAD_DIM, h * tq:(h + 1) * tq] = qt_ref[h * HEAD_DIM:(h + 1) * HEAD_DIM, :]
        qz_ref[HEAD_DIM:K_PAD, h * tq:(h + 1) * tq] = zeros


def _sublane_partial_sum(e):
    return jnp.sum(e.reshape(e.shape[0] // SUBLANES, SUBLANES, e.shape[1]), axis=0)


def _store_attention(acc, row_sum, o_ref):
    tq = o_ref.shape[0]
    o = acc / jnp.sum(row_sum, axis=0, keepdims=True)
    ot = jnp.concatenate([o[:, h * tq:(h + 1) * tq] for h in range(GQA_GROUP)], axis=0)
    o_ref[...] = ot.T.astype(o_ref.dtype)


def _attn_bounded_kernel(qt_ref, k_ref, vt_ref, o_ref, qz_ref, acc_ref, l_ref, *, kv_tile,
                         unroll):
    n_kv = k_ref.shape[0] // kv_tile
    _stage_queries(qt_ref, qz_ref)
    acc_ref[...] = jnp.zeros(acc_ref.shape, jnp.float32)
    l_ref[...] = jnp.zeros(l_ref.shape, jnp.float32)

    width = qz_ref.shape[1]
    strip = width // COLUMN_STRIPS
    cols = [slice(c * strip, (c + 1) * strip) for c in range(COLUMN_STRIPS)]

    def probs(j, c):
        start = pl.multiple_of(j * kv_tile, kv_tile)
        s = jnp.dot(k_ref[pl.ds(start, kv_tile), :], qz_ref[:, cols[c]],
                    preferred_element_type=jnp.float32)
        e = jnp.exp2(s)
        return e.astype(jnp.bfloat16), _sublane_partial_sum(e)

    def group(jj, carry):
        base = jj * unroll
        acc = [acc_ref[:, cols[c]] for c in range(COLUMN_STRIPS)]
        row_sum = [l_ref[:, cols[c]] for c in range(COLUMN_STRIPS)]
        p = []
        for c in range(COLUMN_STRIPS):
            p_c, part = probs(base, c)
            p.append(p_c)
            row_sum[c] = row_sum[c] + part
        for t in range(unroll):
            p_next = []
            for c in range(COLUMN_STRIPS):
                if t + 1 < unroll:
                    p_c, part = probs(base + t + 1, c)
                    p_next.append(p_c)
                    row_sum[c] = row_sum[c] + part
                acc[c] = acc[c] + jnp.dot(vt_ref[base + t], p[c],
                                          preferred_element_type=jnp.float32)
            p = p_next
        for c in range(COLUMN_STRIPS):
            acc_ref[:, cols[c]] = acc[c]
            l_ref[:, cols[c]] = row_sum[c]
        return carry

    lax.fori_loop(0, n_kv // unroll, group, 0)
    _store_attention(acc_ref[...], l_ref[...], o_ref)


def _attn_online_kernel(qt_ref, k_ref, vt_ref, o_ref, qz_ref, acc_ref, l_ref, m_ref, *,
                        kv_tile, unroll):
    n_kv = k_ref.shape[0] // kv_tile
    _stage_queries(qt_ref, qz_ref)
    m_ref[...] = jnp.full(m_ref.shape, NEG_BIG, jnp.float32)
    acc_ref[...] = jnp.zeros(acc_ref.shape, jnp.float32)
    l_ref[...] = jnp.zeros(l_ref.shape, jnp.float32)

    def scores(j):
        start = pl.multiple_of(j * kv_tile, kv_tile)
        s = jnp.dot(k_ref[pl.ds(start, kv_tile), :], qz_ref[...],
                    preferred_element_type=jnp.float32)
        return s, jnp.max(s, axis=0, keepdims=True)

    def group(jj, carry):
        base = jj * unroll
        s, s_max = scores(base)
        for t in range(unroll):
            if t + 1 < unroll:
                s_next, s_max_next = scores(base + t + 1)
            m_prev = m_ref[...]
            m_new = jnp.maximum(m_prev, s_max)
            e = jnp.exp2(s - m_new)
            pv = jnp.dot(vt_ref[base + t], e.astype(jnp.bfloat16),
                         preferred_element_type=jnp.float32)
            alpha = jnp.exp2(m_prev - m_new)
            acc_ref[...] = alpha * acc_ref[...] + pv
            l_ref[...] = alpha * l_ref[...] + _sublane_partial_sum(e)
            m_ref[...] = m_new
            if t + 1 < unroll:
                s, s_max = s_next, s_max_next
        return carry

    lax.fori_loop(0, n_kv // unroll, group, 0)
    _store_attention(acc_ref[...], l_ref[...], o_ref)


def _attn_call(qt, k, vt, *, bounded, batch, seq_len, tq, kv_tile):
    n_tok = batch * seq_len
    nq = seq_len // tq
    kv_blocks = seq_len // kv_tile
    width = GQA_GROUP * tq
    unroll = _tile(kv_blocks, KV_UNROLL)
    body = _attn_bounded_kernel if bounded else _attn_online_kernel
    scratch = [pltpu.VMEM((K_PAD, width), jnp.bfloat16),
               pltpu.VMEM((HEAD_DIM, width), jnp.float32),
               pltpu.VMEM((SUBLANES, width), jnp.float32)]
    if not bounded:
        scratch.append(pltpu.VMEM((1, width), jnp.float32))
    return pl.pallas_call(
        functools.partial(body, kv_tile=kv_tile, unroll=unroll),
        grid=(batch, N_KV_HEADS, nq),
        in_specs=[
            pl.BlockSpec((GQA_GROUP * HEAD_DIM, tq), lambda b, g, i: (g, b * nq + i)),
            pl.BlockSpec((seq_len, K_PAD), lambda b, g, i: (b, g)),
            pl.BlockSpec((None, kv_blocks, HEAD_DIM, kv_tile), lambda b, g, i: (g, b, 0, 0)),
        ],
        out_specs=pl.BlockSpec((tq, GQA_GROUP * HEAD_DIM), lambda b, g, i: (b * nq + i, g)),
        out_shape=jax.ShapeDtypeStruct((n_tok, D_MODEL), jnp.bfloat16),
        scratch_shapes=scratch,
        compiler_params=_params(("parallel", "parallel", "arbitrary")),
        name="attn_bounded" if bounded else "attn_online",
    )(qt, k, vt)


def _out_kernel(x_ref, attn_ref, sz_ref, sga_ref, cg_ref, wao_ref, wo_ref, g_ref, b_ref,
                y_ref):
    f32 = jnp.float32
    a_in = (attn_ref[...].astype(f32) * sz_ref[...].astype(f32)).astype(jnp.bfloat16)
    a_out = jnp.dot(a_in, wao_ref[...], preferred_element_type=jnp.float32)
    merged = sga_ref[...].astype(f32) * a_out + cg_ref[...]
    out = jnp.dot(merged.astype(jnp.bfloat16), wo_ref[...],
                  preferred_element_type=jnp.float32)
    r = DN_ALPHA * x_ref[...] + out
    mu = jnp.mean(r, axis=-1, keepdims=True)
    d = r - mu
    var = jnp.mean(d * d, axis=-1, keepdims=True)
    y_ref[...] = d * lax.rsqrt(var + LN_EPS) * g_ref[...] + b_ref[...]


def _out_call(x2, attn, sz, sga, cg, w_attn_out, w_o, ln_g, ln_b, *, tm):
    n_tok = x2.shape[0]
    row_spec = pl.BlockSpec((tm, D_MODEL), lambda i: (i, 0))
    return pl.pallas_call(
        _out_kernel,
        grid=(n_tok // tm,),
        in_specs=[row_spec] * 5 + [_full(w_attn_out.shape), _full(w_o.shape),
                                    _full(ln_g.shape), _full(ln_b.shape)],
        out_specs=row_spec,
        out_shape=jax.ShapeDtypeStruct((n_tok, D_MODEL), jnp.float32),
        compiler_params=_params(("parallel",)),
        name="out",
    )(x2, attn, sz, sga, cg, w_attn_out, w_o, ln_g, ln_b)


def _axial_tables_t(seq_len):
    rows = seq_len // GRID_W
    row = jnp.repeat(jnp.arange(rows, dtype=jnp.float32), GRID_W)
    col = jnp.tile(jnp.arange(GRID_W, dtype=jnp.float32), rows)
    inv_freq = ROPE_THETA ** (-jnp.arange(0, ROPE_HALF, 2, dtype=jnp.float32) / ROPE_HALF)
    ang_r = row[:, None] * inv_freq
    ang_c = col[:, None] * inv_freq
    return tuple(t.T for t in (jnp.cos(ang_r), jnp.sin(ang_r), jnp.cos(ang_c), jnp.sin(ang_c)))


def _tile(n, want):
    t = min(n, want)
    assert n % t == 0, (n, t)
    return t


def _layer(x, p):
    batch, seq_len, _ = x.shape
    x2 = x.reshape(batch * seq_len, D_MODEL)
    tm = _tile(seq_len, TOKEN_TILE)
    tq = _tile(seq_len, Q_TILE)
    kv_tile = _tile(tm, KV_TILE)
    tabs = _axial_tables_t(seq_len)
    qt, k, vt = _qkv_call(x2, p["w_qkv_t"], p["b_qkv"], p["q_gain"], p["k_gain"], tabs,
                          seq_len=seq_len, tm=tm, kv_tile=kv_tile)
    sz, sga, cg = _gates_call(x2, p["w_rest"], p["b_rest"], p["conv_w"], p["conv_b"],
                              p["w_conv_out"], seq_len=seq_len, tm=tm)
    attend = functools.partial(_attn_call, batch=batch, seq_len=seq_len, tq=tq,
                               kv_tile=kv_tile)
    attn = lax.cond(p["logit_bound"] <= SAFE_LOGIT_BOUND,
                    functools.partial(attend, bounded=True),
                    functools.partial(attend, bounded=False), qt, k, vt)
    y = _out_call(x2, attn, sz, sga, cg, p["w_attn_out"], p["w_o"], p["ln_g"], p["ln_b"],
                  tm=tm)
    return y.reshape(batch, seq_len, D_MODEL)


def _prepare(w_in, b_in, q_gain, k_gain, conv_w, conv_b, w_attn_out, w_conv_out, w_o,
             ln_g, ln_b):
    bf = jnp.bfloat16
    return {
        "w_qkv_t": w_in[:, :QKV_WIDTH].T.astype(bf),
        "b_qkv": b_in[:QKV_WIDTH].reshape(QKV_WIDTH, 1),
        "q_gain": (q_gain * (SCORE_SCALE * LOG2_E)).reshape(HEAD_DIM, 1),
        "k_gain": k_gain.reshape(HEAD_DIM, 1),
        "logit_bound": (math.sqrt(HEAD_DIM) * jnp.max(jnp.abs(q_gain))
                        * jnp.max(jnp.abs(k_gain))),
        "w_rest": w_in[:, QKV_WIDTH:].astype(bf),
        "b_rest": b_in[QKV_WIDTH:].reshape(1, -1),
        "conv_w": conv_w,
        "conv_b": conv_b.reshape(1, D_MODEL),
        "w_conv_out": w_conv_out.astype(bf),
        "w_attn_out": w_attn_out.astype(bf),
        "w_o": w_o.astype(bf),
        "ln_g": ln_g.reshape(1, D_MODEL),
        "ln_b": ln_b.reshape(1, D_MODEL),
    }


def kernel(x_prompt, x_sample, w_in, b_in, q_gain, k_gain, conv_w, conv_b, w_attn_out,
           w_conv_out, w_o, ln_g, ln_b):
    depth = w_in.shape[0]
    y_prompt, y_sample = x_prompt, x_sample
    for l in range(depth):
        p = _prepare(w_in[l], b_in[l], q_gain[l], k_gain[l], conv_w[l], conv_b[l],
                     w_attn_out[l], w_conv_out[l], w_o[l], ln_g[l], ln_b[l])
        y_prompt = _layer(y_prompt, p)
        y_sample = _layer(y_sample, p)
    return (y_prompt, y_sample)
```

```python
import functools
import math

import jax
import jax.numpy as jnp
from jax import lax
from jax.experimental import pallas as pl
from jax.experimental.pallas import tpu as pltpu

D_MODEL = 1024
HEAD_DIM = 64
N_HEADS = 16
N_KV_HEADS = 4
GQA_GROUP = N_HEADS // N_KV_HEADS
KV_WIDTH = N_KV_HEADS * HEAD_DIM
QKV_WIDTH = D_MODEL + 2 * KV_WIDTH
GRID_W = 64
ROPE_THETA = 10000.0
ROPE_HALF = HEAD_DIM // 2
ROPE_PAIR = ROPE_HALF // 2
NORM_EPS = 1e-6
LN_EPS = 1e-5
DN_ALPHA = 2.0 ** 0.25
SCORE_SCALE = 1.0 / math.sqrt(HEAD_DIM)
LOG2_E = math.log2(math.e)

SUBLANES = 8
VMEM_LIMIT_BYTES = 56 * 1024 * 1024

FP8 = jnp.float8_e4m3fn
LO_SCALE = 16.0
SPLIT_DEPTH = 4 * HEAD_DIM
OPERAND_PEAK = 16.0
NEG_BIG = -1e30
SAFE_LOGIT_BOUND = 30.0

TOKEN_TILE = 512
HALO = SUBLANES
Q_TILE = 256
KV_TILE = 512
KV_UNROLL = 8


def _params(sem):
    return pltpu.CompilerParams(dimension_semantics=sem,
                                vmem_limit_bytes=VMEM_LIMIT_BYTES)


def _full(shape):
    return pl.BlockSpec(shape, lambda *_: (0,) * len(shape))


def _norm_rope_t(u, gain, cr, sr, cc, sc):
    ms = jnp.mean(u * u, axis=0, keepdims=True)
    u = u * lax.rsqrt(ms + NORM_EPS) * gain
    p = ROPE_PAIR
    r1, r2, c1, c2 = u[0:p], u[p:2 * p], u[2 * p:3 * p], u[3 * p:4 * p]
    return jnp.concatenate([r1 * cr - r2 * sr, r2 * cr + r1 * sr,
                            c1 * cc - c2 * sc, c2 * cc + c1 * sc], axis=0)


def _split_fp8(u):
    hi = u.astype(FP8).astype(jnp.float32)
    lo = ((u - hi) * LO_SCALE).astype(FP8).astype(jnp.float32)
    return hi, lo


def _qkv_kernel(x_ref, w_ref, b_ref, qg_ref, kg_ref, cr_ref, sr_ref, cc_ref, sc_ref,
                qt_ref, k_ref, vt_ref, *, kv_tile):
    tm = x_ref.shape[0]
    xb = x_ref[...].astype(jnp.bfloat16)
    pt = lax.dot_general(w_ref[...], xb, (((1,), (1,)), ((), ())),
                         preferred_element_type=jnp.float32) + b_ref[...]
    cr, sr, cc, sc = cr_ref[...], sr_ref[...], cc_ref[...], sc_ref[...]
    qg, kg = qg_ref[...], kg_ref[...]
    for h in range(N_HEADS):
        u = pt[h * HEAD_DIM:(h + 1) * HEAD_DIM]
        hi, lo = _split_fp8(_norm_rope_t(u, qg, cr, sr, cc, sc))
        parts = jnp.concatenate([hi, hi * (1.0 / LO_SCALE), lo, lo * (1.0 / LO_SCALE)], axis=0)
        qt_ref[h * SPLIT_DEPTH:(h + 1) * SPLIT_DEPTH, :] = parts.astype(FP8)
    kparts = []
    for g in range(N_KV_HEADS):
        u = pt[D_MODEL + g * HEAD_DIM:D_MODEL + (g + 1) * HEAD_DIM]
        hi, lo = _split_fp8(_norm_rope_t(u, kg, cr, sr, cc, sc))
        kparts += [hi, lo, hi * (1.0 / LO_SCALE), lo * (1.0 / LO_SCALE)]
    kt = jnp.concatenate(kparts, axis=0)
    k_ref[...] = kt.T.astype(FP8)
    for g in range(N_KV_HEADS):
        v = pt[D_MODEL + KV_WIDTH + g * HEAD_DIM:D_MODEL + KV_WIDTH + (g + 1) * HEAD_DIM]
        v = v.astype(jnp.bfloat16)
        for c in range(tm // kv_tile):
            vt_ref[g, c] = v[:, c * kv_tile:(c + 1) * kv_tile]


def _qkv_call(x2, w_qkv_t, b_qkv, qg, kg, tabs, *, seq_len, tm, kv_tile):
    n_tok = x2.shape[0]
    tiles_per_seq = seq_len // tm
    tab_spec = pl.BlockSpec((ROPE_PAIR, tm), lambda i: (0, i % tiles_per_seq))
    return pl.pallas_call(
        functools.partial(_qkv_kernel, kv_tile=kv_tile),
        grid=(n_tok // tm,),
        in_specs=[
            pl.BlockSpec((tm, D_MODEL), lambda i: (i, 0)),
            _full((QKV_WIDTH, D_MODEL)),
            _full((QKV_WIDTH, 1)),
            _full((HEAD_DIM, 1)),
            _full((HEAD_DIM, 1)),
            tab_spec, tab_spec, tab_spec, tab_spec,
        ],
        out_specs=[
            pl.BlockSpec((N_HEADS * SPLIT_DEPTH, tm), lambda i: (0, i)),
            pl.BlockSpec((tm, N_KV_HEADS * SPLIT_DEPTH), lambda i: (i, 0)),
            pl.BlockSpec((N_KV_HEADS, tm // kv_tile, HEAD_DIM, kv_tile),
                         lambda i: (0, i, 0, 0)),
        ],
        out_shape=[
            jax.ShapeDtypeStruct((N_HEADS * SPLIT_DEPTH, n_tok), FP8),
            jax.ShapeDtypeStruct((n_tok, N_KV_HEADS * SPLIT_DEPTH), FP8),
            jax.ShapeDtypeStruct((N_KV_HEADS, n_tok // kv_tile, HEAD_DIM, kv_tile),
                                 jnp.bfloat16),
        ],
        compiler_params=_params(("parallel",)),
        name="qkv",
    )(x2, w_qkv_t, b_qkv, qg, kg, *tabs)


def _silu(x):
    return x * jax.nn.sigmoid(x)


def _gates_kernel(x_ref, xp_ref, xn_ref, w_ref, b_ref, cw_ref, cb_ref, wco_ref,
                  sz_ref, sga_ref, cg_ref, *, tiles_per_seq):
    tm = x_ref.shape[0]
    i = pl.program_id(0)
    x = x_ref[...]
    xb = x.astype(jnp.bfloat16)
    xcat = jnp.concatenate([xp_ref[...], x, xn_ref[...]], axis=0).astype(jnp.bfloat16)

    def proj(lhs, c):
        sl = slice(c * D_MODEL, (c + 1) * D_MODEL)
        return jnp.dot(lhs, w_ref[:, sl], preferred_element_type=jnp.float32) + b_ref[:, sl]

    sz_ref[...] = _silu(proj(xb, 0)).astype(sz_ref.dtype)
    sga_ref[...] = jax.nn.sigmoid(proj(xb, 5)).astype(sga_ref.dtype)

    u = proj(xcat, 2) * proj(xcat, 3)
    rows = lax.broadcasted_iota(jnp.int32, u.shape, 0)
    first = (i % tiles_per_seq) == 0
    last = (i % tiles_per_seq) == tiles_per_seq - 1
    outside = ((rows < HALO) & first) | ((rows >= tm + HALO) & last)
    u = jnp.where(outside, 0.0, u)
    n = tm + 2 * HALO
    u_prev = pltpu.roll(u, 1, axis=0)[HALO:HALO + tm]
    u_next = pltpu.roll(u, n - 1, axis=0)[HALO:HALO + tm]
    cw = cw_ref[...]
    conv = (u_prev * cw[0:1] + u[HALO:HALO + tm] * cw[1:2] + u_next * cw[2:3]
            + cb_ref[...])
    act = proj(xb, 1) * conv * _silu(proj(xb, 4))
    c_out = jnp.dot(act.astype(jnp.bfloat16), wco_ref[...],
                    preferred_element_type=jnp.float32)
    cg_ref[...] = jax.nn.sigmoid(proj(xb, 6)) * c_out


def _gates_call(x2, w_rest, b_rest, conv_w, conv_b, w_conv_out, *, seq_len, tm):
    n_tok = x2.shape[0]
    halo_blocks = tm // HALO
    last_halo = n_tok // HALO - 1
    row_spec = pl.BlockSpec((tm, D_MODEL), lambda i: (i, 0))
    gate = jax.ShapeDtypeStruct((n_tok, D_MODEL), jnp.bfloat16)
    branch = jax.ShapeDtypeStruct((n_tok, D_MODEL), jnp.float32)
    return pl.pallas_call(
        functools.partial(_gates_kernel, tiles_per_seq=seq_len // tm),
        grid=(n_tok // tm,),
        in_specs=[
            row_spec,
            pl.BlockSpec((HALO, D_MODEL),
                         lambda i: (jnp.maximum(i * halo_blocks - 1, 0), 0)),
            pl.BlockSpec((HALO, D_MODEL),
                         lambda i: (jnp.minimum((i + 1) * halo_blocks, last_halo), 0)),
            _full(w_rest.shape),
            _full(b_rest.shape),
            _full(conv_w.shape),
            _full(conv_b.shape),
            _full(w_conv_out.shape),
        ],
        out_specs=[row_spec, row_spec, row_spec],
        out_shape=[gate, gate, branch],
        compiler_params=_params(("parallel",)),
        name="gates",
    )(x2, x2, x2, w_rest, b_rest, conv_w, conv_b, w_conv_out)


def _scores(k_ref, qt_ref, scale, j, h, kv_tile):
    start = pl.multiple_of(j * kv_tile, kv_tile)
    s = jnp.dot(k_ref[pl.ds(start, kv_tile), :],
                qt_ref[h * SPLIT_DEPTH:(h + 1) * SPLIT_DEPTH, :],
                preferred_element_type=jnp.float32)
    return s * scale


def _sublane_partial_sum(e):
    return jnp.sum(e.reshape(e.shape[0] // SUBLANES, SUBLANES, e.shape[1]), axis=0)


def _store_attention(acc, row_sum, o_ref):
    tq = o_ref.shape[0]
    o = acc / jnp.sum(row_sum, axis=0, keepdims=True)
    ot = jnp.concatenate([o[:, h * tq:(h + 1) * tq] for h in range(GQA_GROUP)], axis=0)
    o_ref[...] = ot.T.astype(o_ref.dtype)


def _attn_bounded_kernel(scale_ref, qt_ref, k_ref, vt_ref, o_ref, acc_ref, l_ref, *, kv_tile,
                         unroll):
    n_kv = k_ref.shape[0] // kv_tile
    tq = qt_ref.shape[1]
    scale = scale_ref[0]
    acc_ref[...] = jnp.zeros(acc_ref.shape, jnp.float32)
    l_ref[...] = jnp.zeros(l_ref.shape, jnp.float32)
    cols = [slice(h * tq, (h + 1) * tq) for h in range(GQA_GROUP)]

    def probs(j, h):
        e = jnp.exp2(_scores(k_ref, qt_ref, scale, j, h, kv_tile))
        return e.astype(jnp.bfloat16), _sublane_partial_sum(e)

    def group(jj, carry):
        base = jj * unroll
        acc = [acc_ref[:, cols[h]] for h in range(GQA_GROUP)]
        row_sum = [l_ref[:, cols[h]] for h in range(GQA_GROUP)]
        p = []
        for h in range(GQA_GROUP):
            p_h, part = probs(base, h)
            p.append(p_h)
            row_sum[h] = row_sum[h] + part
        for t in range(unroll):
            p_next = []
            for h in range(GQA_GROUP):
                if t + 1 < unroll:
                    p_h, part = probs(base + t + 1, h)
                    p_next.append(p_h)
                    row_sum[h] = row_sum[h] + part
                acc[h] = acc[h] + jnp.dot(vt_ref[base + t], p[h],
                                          preferred_element_type=jnp.float32)
            p = p_next
        for h in range(GQA_GROUP):
            acc_ref[:, cols[h]] = acc[h]
            l_ref[:, cols[h]] = row_sum[h]
        return carry

    lax.fori_loop(0, n_kv // unroll, group, 0)
    _store_attention(acc_ref[...], l_ref[...], o_ref)


def _attn_online_kernel(scale_ref, qt_ref, k_ref, vt_ref, o_ref, acc_ref, l_ref, m_ref, *,
                        kv_tile, unroll):
    n_kv = k_ref.shape[0] // kv_tile
    scale = scale_ref[0]
    m_ref[...] = jnp.full(m_ref.shape, NEG_BIG, jnp.float32)
    acc_ref[...] = jnp.zeros(acc_ref.shape, jnp.float32)
    l_ref[...] = jnp.zeros(l_ref.shape, jnp.float32)

    def scores(j):
        s = jnp.concatenate([_scores(k_ref, qt_ref, scale, j, h, kv_tile)
                             for h in range(GQA_GROUP)], axis=1)
        return s, jnp.max(s, axis=0, keepdims=True)

    def group(jj, carry):
        base = jj * unroll
        s, s_max = scores(base)
        for t in range(unroll):
            if t + 1 < unroll:
                s_next, s_max_next = scores(base + t + 1)
            m_prev = m_ref[...]
            m_new = jnp.maximum(m_prev, s_max)
            e = jnp.exp2(s - m_new)
            pv = jnp.dot(vt_ref[base + t], e.astype(jnp.bfloat16),
                         preferred_element_type=jnp.float32)
            alpha = jnp.exp2(m_prev - m_new)
            acc_ref[...] = alpha * acc_ref[...] + pv
            l_ref[...] = alpha * l_ref[...] + _sublane_partial_sum(e)
            m_ref[...] = m_new
            if t + 1 < unroll:
                s, s_max = s_next, s_max_next
        return carry

    lax.fori_loop(0, n_kv // unroll, group, 0)
    _store_attention(acc_ref[...], l_ref[...], o_ref)


def _attn_call(score_scale, qt, k, vt, *, bounded, batch, seq_len, tq, kv_tile):
    n_tok = batch * seq_len
    nq = seq_len // tq
    kv_blocks = seq_len // kv_tile
    width = GQA_GROUP * tq
    unroll = _tile(kv_blocks, KV_UNROLL)
    body = _attn_bounded_kernel if bounded else _attn_online_kernel
    scratch = [pltpu.VMEM((HEAD_DIM, width), jnp.float32),
               pltpu.VMEM((SUBLANES, width), jnp.float32)]
    if not bounded:
        scratch.append(pltpu.VMEM((1, width), jnp.float32))
    return pl.pallas_call(
        functools.partial(body, kv_tile=kv_tile, unroll=unroll),
        grid=(batch, N_KV_HEADS, nq),
        in_specs=[
            pl.BlockSpec(memory_space=pltpu.SMEM),
            pl.BlockSpec((GQA_GROUP * SPLIT_DEPTH, tq), lambda b, g, i: (g, b * nq + i)),
            pl.BlockSpec((seq_len, SPLIT_DEPTH), lambda b, g, i: (b, g)),
            pl.BlockSpec((None, kv_blocks, HEAD_DIM, kv_tile), lambda b, g, i: (g, b, 0, 0)),
        ],
        out_specs=pl.BlockSpec((tq, GQA_GROUP * HEAD_DIM), lambda b, g, i: (b * nq + i, g)),
        out_shape=jax.ShapeDtypeStruct((n_tok, D_MODEL), jnp.bfloat16),
        scratch_shapes=scratch,
        compiler_params=_params(("parallel", "parallel", "arbitrary")),
        name="attn_bounded" if bounded else "attn_online",
    )(score_scale, qt, k, vt)


def _out_kernel(x_ref, attn_ref, sz_ref, sga_ref, cg_ref, wao_ref, wo_ref, g_ref, b_ref,
                y_ref):
    f32 = jnp.float32
    a_in = (attn_ref[...].astype(f32) * sz_ref[...].astype(f32)).astype(jnp.bfloat16)
    a_out = jnp.dot(a_in, wao_ref[...], preferred_element_type=jnp.float32)
    merged = sga_ref[...].astype(f32) * a_out + cg_ref[...]
    out = jnp.dot(merged.astype(jnp.bfloat16), wo_ref[...],
                  preferred_element_type=jnp.float32)
    r = DN_ALPHA * x_ref[...] + out
    mu = jnp.mean(r, axis=-1, keepdims=True)
    d = r - mu
    var = jnp.mean(d * d, axis=-1, keepdims=True)
    y_ref[...] = d * lax.rsqrt(var + LN_EPS) * g_ref[...] + b_ref[...]


def _out_call(x2, attn, sz, sga, cg, w_attn_out, w_o, ln_g, ln_b, *, tm):
    n_tok = x2.shape[0]
    row_spec = pl.BlockSpec((tm, D_MODEL), lambda i: (i, 0))
    return pl.pallas_call(
        _out_kernel,
        grid=(n_tok // tm,),
        in_specs=[row_spec] * 5 + [_full(w_attn_out.shape), _full(w_o.shape),
                                    _full(ln_g.shape), _full(ln_b.shape)],
        out_specs=row_spec,
        out_shape=jax.ShapeDtypeStruct((n_tok, D_MODEL), jnp.float32),
        compiler_params=_params(("parallel",)),
        name="out",
    )(x2, attn, sz, sga, cg, w_attn_out, w_o, ln_g, ln_b)


def _axial_tables_t(seq_len):
    rows = seq_len // GRID_W
    row = jnp.repeat(jnp.arange(rows, dtype=jnp.float32), GRID_W)
    col = jnp.tile(jnp.arange(GRID_W, dtype=jnp.float32), rows)
    inv_freq = ROPE_THETA ** (-jnp.arange(0, ROPE_HALF, 2, dtype=jnp.float32) / ROPE_HALF)
    ang_r = row[:, None] * inv_freq
    ang_c = col[:, None] * inv_freq
    return tuple(t.T for t in (jnp.cos(ang_r), jnp.sin(ang_r), jnp.cos(ang_c), jnp.sin(ang_c)))


def _tile(n, want):
    t = min(n, want)
    assert n % t == 0, (n, t)
    return t


def _layer(x, p):
    batch, seq_len, _ = x.shape
    x2 = x.reshape(batch * seq_len, D_MODEL)
    tm = _tile(seq_len, TOKEN_TILE)
    tq = _tile(seq_len, Q_TILE)
    kv_tile = _tile(tm, KV_TILE)
    tabs = _axial_tables_t(seq_len)
    qt, k, vt = _qkv_call(x2, p["w_qkv_t"], p["b_qkv"], p["q_gain"], p["k_gain"], tabs,
                          seq_len=seq_len, tm=tm, kv_tile=kv_tile)
    sz, sga, cg = _gates_call(x2, p["w_rest"], p["b_rest"], p["conv_w"], p["conv_b"],
                              p["w_conv_out"], seq_len=seq_len, tm=tm)
    attend = functools.partial(_attn_call, batch=batch, seq_len=seq_len, tq=tq,
                               kv_tile=kv_tile)
    score_scale = p["score_scale"].reshape(1)
    attn = lax.cond(p["logit_bound"] <= SAFE_LOGIT_BOUND,
                    functools.partial(attend, bounded=True),
                    functools.partial(attend, bounded=False), score_scale, qt, k, vt)
    y = _out_call(x2, attn, sz, sga, cg, p["w_attn_out"], p["w_o"], p["ln_g"], p["ln_b"],
                  tm=tm)
    return y.reshape(batch, seq_len, D_MODEL)


def _prepare(w_in, b_in, q_gain, k_gain, conv_w, conv_b, w_attn_out, w_conv_out, w_o,
             ln_g, ln_b):
    bf = jnp.bfloat16
    tiny = jnp.finfo(jnp.float32).tiny
    q_peak = jnp.maximum(jnp.max(jnp.abs(q_gain)), tiny)
    k_peak = jnp.maximum(jnp.max(jnp.abs(k_gain)), tiny)
    return {
        "w_qkv_t": w_in[:, :QKV_WIDTH].T.astype(bf),
        "b_qkv": b_in[:QKV_WIDTH].reshape(QKV_WIDTH, 1),
        "q_gain": (q_gain * (OPERAND_PEAK / q_peak)).reshape(HEAD_DIM, 1),
        "k_gain": (k_gain * (OPERAND_PEAK / k_peak)).reshape(HEAD_DIM, 1),
        "score_scale": (SCORE_SCALE * LOG2_E / OPERAND_PEAK ** 2) * q_peak * k_peak,
        "logit_bound": math.sqrt(HEAD_DIM) * q_peak * k_peak,
        "w_rest": w_in[:, QKV_WIDTH:].astype(bf),
        "b_rest": b_in[QKV_WIDTH:].reshape(1, -1),
        "conv_w": conv_w,
        "conv_b": conv_b.reshape(1, D_MODEL),
        "w_conv_out": w_conv_out.astype(bf),
        "w_attn_out": w_attn_out.astype(bf),
        "w_o": w_o.astype(bf),
        "ln_g": ln_g.reshape(1, D_MODEL),
        "ln_b": ln_b.reshape(1, D_MODEL),
    }


def kernel(x_prompt, x_sample, w_in, b_in, q_gain, k_gain, conv_w, conv_b, w_attn_out,
           w_conv_out, w_o, ln_g, ln_b):
    depth = w_in.shape[0]
    y_prompt, y_sample = x_prompt, x_sample
    for l in range(depth):
        p = _prepare(w_in[l], b_in[l], q_gain[l], k_gain[l], conv_w[l], conv_b[l],
                     w_attn_out[l], w_conv_out[l], w_o[l], ln_g[l], ln_b[l])
        y_prompt = _layer(y_prompt, p)
        y_sample = _layer(y_sample, p)
    return (y_prompt, y_sample)
```

```python
import functools
import math

import jax
import jax.numpy as jnp
from jax import lax
from jax.experimental import pallas as pl
from jax.experimental.pallas import tpu as pltpu

D_MODEL = 1024
HEAD_DIM = 64
N_HEADS = 16
N_KV_HEADS = 4
GQA_GROUP = N_HEADS // N_KV_HEADS
KV_WIDTH = N_KV_HEADS * HEAD_DIM
QKV_WIDTH = D_MODEL + 2 * KV_WIDTH
GRID_W = 64
ROPE_THETA = 10000.0
ROPE_HALF = HEAD_DIM // 2
ROPE_PAIR = ROPE_HALF // 2
NORM_EPS = 1e-6
LN_EPS = 1e-5
DN_ALPHA = 2.0 ** 0.25
SCORE_SCALE = 1.0 / math.sqrt(HEAD_DIM)
LOG2_E = math.log2(math.e)

SUBLANES = 8
VMEM_LIMIT_BYTES = 56 * 1024 * 1024

FP8 = jnp.float8_e4m3fn
LO_SCALE = 16.0
SPLIT_DEPTH = 4 * HEAD_DIM
OPERAND_PEAK = 16.0
NEG_BIG = -1e30
SAFE_LOGIT_BOUND = 30.0

TOKEN_TILE = 512
HALO = SUBLANES
Q_TILE = 256
Q_BLOCK = 512
KV_TILE = 512
KV_UNROLL = 16


def _params(sem):
    return pltpu.CompilerParams(dimension_semantics=sem,
                                vmem_limit_bytes=VMEM_LIMIT_BYTES)


def _full(shape):
    return pl.BlockSpec(shape, lambda *_: (0,) * len(shape))


def _norm_rope_t(u, gain, cr, sr, cc, sc):
    ms = jnp.mean(u * u, axis=0, keepdims=True)
    u = u * lax.rsqrt(ms + NORM_EPS) * gain
    p = ROPE_PAIR
    r1, r2, c1, c2 = u[0:p], u[p:2 * p], u[2 * p:3 * p], u[3 * p:4 * p]
    return jnp.concatenate([r1 * cr - r2 * sr, r2 * cr + r1 * sr,
                            c1 * cc - c2 * sc, c2 * cc + c1 * sc], axis=0)


def _split_fp8(u):
    hi = u.astype(FP8).astype(jnp.float32)
    lo = ((u - hi) * LO_SCALE).astype(FP8).astype(jnp.float32)
    return hi, lo


def _qkv_kernel(x_ref, w_ref, b_ref, qg_ref, kg_ref, cr_ref, sr_ref, cc_ref, sc_ref,
                qt_ref, k_ref, vt_ref, *, kv_tile):
    tm = x_ref.shape[0]
    xb = x_ref[...].astype(jnp.bfloat16)
    pt = lax.dot_general(w_ref[...], xb, (((1,), (1,)), ((), ())),
                         preferred_element_type=jnp.float32) + b_ref[...]
    cr, sr, cc, sc = cr_ref[...], sr_ref[...], cc_ref[...], sc_ref[...]
    qg, kg = qg_ref[...], kg_ref[...]
    for h in range(N_HEADS):
        u = pt[h * HEAD_DIM:(h + 1) * HEAD_DIM]
        hi, lo = _split_fp8(_norm_rope_t(u, qg, cr, sr, cc, sc))
        parts = jnp.concatenate([hi, hi * (1.0 / LO_SCALE), lo, lo * (1.0 / LO_SCALE)], axis=0)
        qt_ref[h * SPLIT_DEPTH:(h + 1) * SPLIT_DEPTH, :] = parts.astype(FP8)
    kparts = []
    for g in range(N_KV_HEADS):
        u = pt[D_MODEL + g * HEAD_DIM:D_MODEL + (g + 1) * HEAD_DIM]
        hi, lo = _split_fp8(_norm_rope_t(u, kg, cr, sr, cc, sc))
        kparts += [hi, lo, hi * (1.0 / LO_SCALE), lo * (1.0 / LO_SCALE)]
    kt = jnp.concatenate(kparts, axis=0)
    k_ref[...] = kt.T.astype(FP8)
    for g in range(N_KV_HEADS):
        v = pt[D_MODEL + KV_WIDTH + g * HEAD_DIM:D_MODEL + KV_WIDTH + (g + 1) * HEAD_DIM]
        v = v.astype(jnp.bfloat16)
        for c in range(tm // kv_tile):
            vt_ref[g, c] = v[:, c * kv_tile:(c + 1) * kv_tile]


def _qkv_call(x2, w_qkv_t, b_qkv, qg, kg, tabs, *, seq_len, tm, kv_tile):
    n_tok = x2.shape[0]
    tiles_per_seq = seq_len // tm
    tab_spec = pl.BlockSpec((ROPE_PAIR, tm), lambda i: (0, i % tiles_per_seq))
    return pl.pallas_call(
        functools.partial(_qkv_kernel, kv_tile=kv_tile),
        grid=(n_tok // tm,),
        in_specs=[
            pl.BlockSpec((tm, D_MODEL), lambda i: (i, 0)),
            _full((QKV_WIDTH, D_MODEL)),
            _full((QKV_WIDTH, 1)),
            _full((HEAD_DIM, 1)),
            _full((HEAD_DIM, 1)),
            tab_spec, tab_spec, tab_spec, tab_spec,
        ],
        out_specs=[
            pl.BlockSpec((N_HEADS * SPLIT_DEPTH, tm), lambda i: (0, i)),
            pl.BlockSpec((tm, N_KV_HEADS * SPLIT_DEPTH), lambda i: (i, 0)),
            pl.BlockSpec((N_KV_HEADS, tm // kv_tile, HEAD_DIM, kv_tile),
                         lambda i: (0, i, 0, 0)),
        ],
        out_shape=[
            jax.ShapeDtypeStruct((N_HEADS * SPLIT_DEPTH, n_tok), FP8),
            jax.ShapeDtypeStruct((n_tok, N_KV_HEADS * SPLIT_DEPTH), FP8),
            jax.ShapeDtypeStruct((N_KV_HEADS, n_tok // kv_tile, HEAD_DIM, kv_tile),
                                 jnp.bfloat16),
        ],
        compiler_params=_params(("parallel",)),
        name="qkv",
    )(x2, w_qkv_t, b_qkv, qg, kg, *tabs)


def _silu(x):
    return x * jax.nn.sigmoid(x)


def _gates_kernel(x_ref, xp_ref, xn_ref, w_ref, b_ref, cw_ref, cb_ref, wco_ref,
                  sz_ref, sga_ref, cg_ref, *, tiles_per_seq):
    tm = x_ref.shape[0]
    i = pl.program_id(0)
    x = x_ref[...]
    xb = x.astype(jnp.bfloat16)
    xcat = jnp.concatenate([xp_ref[...], x, xn_ref[...]], axis=0).astype(jnp.bfloat16)

    def proj(lhs, c):
        sl = slice(c * D_MODEL, (c + 1) * D_MODEL)
        return jnp.dot(lhs, w_ref[:, sl], preferred_element_type=jnp.float32) + b_ref[:, sl]

    sz_ref[...] = _silu(proj(xb, 0)).astype(sz_ref.dtype)
    sga_ref[...] = jax.nn.sigmoid(proj(xb, 5)).astype(sga_ref.dtype)

    u = proj(xcat, 2) * proj(xcat, 3)
    rows = lax.broadcasted_iota(jnp.int32, u.shape, 0)
    first = (i % tiles_per_seq) == 0
    last = (i % tiles_per_seq) == tiles_per_seq - 1
    outside = ((rows < HALO) & first) | ((rows >= tm + HALO) & last)
    u = jnp.where(outside, 0.0, u)
    n = tm + 2 * HALO
    u_prev = pltpu.roll(u, 1, axis=0)[HALO:HALO + tm]
    u_next = pltpu.roll(u, n - 1, axis=0)[HALO:HALO + tm]
    cw = cw_ref[...]
    conv = (u_prev * cw[0:1] + u[HALO:HALO + tm] * cw[1:2] + u_next * cw[2:3]
            + cb_ref[...])
    act = proj(xb, 1) * conv * _silu(proj(xb, 4))
    c_out = jnp.dot(act.astype(jnp.bfloat16), wco_ref[...],
                    preferred_element_type=jnp.float32)
    cg_ref[...] = jax.nn.sigmoid(proj(xb, 6)) * c_out


def _gates_call(x2, w_rest, b_rest, conv_w, conv_b, w_conv_out, *, seq_len, tm):
    n_tok = x2.shape[0]
    halo_blocks = tm // HALO
    last_halo = n_tok // HALO - 1
    row_spec = pl.BlockSpec((tm, D_MODEL), lambda i: (i, 0))
    gate = jax.ShapeDtypeStruct((n_tok, D_MODEL), jnp.bfloat16)
    branch = jax.ShapeDtypeStruct((n_tok, D_MODEL), jnp.float32)
    return pl.pallas_call(
        functools.partial(_gates_kernel, tiles_per_seq=seq_len // tm),
        grid=(n_tok // tm,),
        in_specs=[
            row_spec,
            pl.BlockSpec((HALO, D_MODEL),
                         lambda i: (jnp.maximum(i * halo_blocks - 1, 0), 0)),
            pl.BlockSpec((HALO, D_MODEL),
                         lambda i: (jnp.minimum((i + 1) * halo_blocks, last_halo), 0)),
            _full(w_rest.shape),
            _full(b_rest.shape),
            _full(conv_w.shape),
            _full(conv_b.shape),
            _full(w_conv_out.shape),
        ],
        out_specs=[row_spec, row_spec, row_spec],
        out_shape=[gate, gate, branch],
        compiler_params=_params(("parallel",)),
        name="gates",
    )(x2, x2, x2, w_rest, b_rest, conv_w, conv_b, w_conv_out)


def _scores(k_ref, qt_ref, scale, u, j, h, kv_tile, tq):
    start = pl.multiple_of(j * kv_tile, kv_tile)
    s = jnp.dot(k_ref[pl.ds(start, kv_tile), :],
                qt_ref[h * SPLIT_DEPTH:(h + 1) * SPLIT_DEPTH, u * tq:(u + 1) * tq],
                preferred_element_type=jnp.float32)
    return s * scale


def _sublane_partial_sum(e):
    return jnp.sum(e.reshape(e.shape[0] // SUBLANES, SUBLANES, e.shape[1]), axis=0)


def _store_attention(acc, row_sum, o_ref, u, tq):
    o = [a / jnp.sum(l, axis=0, keepdims=True) for a, l in zip(acc, row_sum)]
    o_ref[u * tq:(u + 1) * tq, :] = jnp.concatenate(o, axis=0).T.astype(o_ref.dtype)


def _attn_bounded_kernel(scale_ref, qt_ref, k_ref, vt_ref, o_ref, acc_ref, l_ref, *, kv_tile,
                         unroll, tq):
    n_kv = k_ref.shape[0] // kv_tile
    n_sub = qt_ref.shape[1] // tq
    scale = scale_ref[0]
    cols = [slice(h * tq, (h + 1) * tq) for h in range(GQA_GROUP)]
    heads = range(GQA_GROUP)

    def probs(u, j, h):
        e = jnp.exp2(_scores(k_ref, qt_ref, scale, u, j, h, kv_tile, tq))
        return e.astype(jnp.bfloat16), _sublane_partial_sum(e)

    def run(items, state, finish):
        def issue(i):
            u, j = items[i]
            out = []
            for h in heads:
                p_h, part = probs(u, j, h)
                state[u][1][h] = state[u][1][h] + part
                out.append(p_h)
                yield out

        p = list(issue(0))[-1]
        for i, (u, j) in enumerate(items):
            ahead = issue(i + 1) if i + 1 < len(items) else None
            p_next = None
            for h in heads:
                if ahead is not None:
                    p_next = next(ahead)
                state[u][0][h] = state[u][0][h] + jnp.dot(
                    vt_ref[j], p[h], preferred_element_type=jnp.float32)
            p = p_next
            if finish and (i + 1 == len(items) or items[i + 1][0] != u):
                _store_attention(state[u][0], state[u][1], o_ref, u, tq)

    def zeros(rows):
        return [jnp.zeros((rows, tq), jnp.float32) for _ in heads]

    if n_sub * n_kv <= unroll:
        items = [(u, j) for u in range(n_sub) for j in range(n_kv)]
        run(items, {u: [zeros(HEAD_DIM), zeros(SUBLANES)] for u in range(n_sub)}, True)
        return

    for u in range(n_sub):
        acc_ref[...] = jnp.zeros(acc_ref.shape, jnp.float32)
        l_ref[...] = jnp.zeros(l_ref.shape, jnp.float32)

        def trip(jj, carry, u=u):
            state = {u: [[acc_ref[:, cols[h]] for h in heads],
                         [l_ref[:, cols[h]] for h in heads]]}
            run([(u, jj * unroll + t) for t in range(unroll)], state, False)
            for h in heads:
                acc_ref[:, cols[h]] = state[u][0][h]
                l_ref[:, cols[h]] = state[u][1][h]
            return carry

        lax.fori_loop(0, n_kv // unroll, trip, 0)
        _store_attention([acc_ref[:, cols[h]] for h in heads],
                         [l_ref[:, cols[h]] for h in heads], o_ref, u, tq)


def _attn_online_kernel(scale_ref, qt_ref, k_ref, vt_ref, o_ref, acc_ref, l_ref, m_ref, *,
                        kv_tile, unroll, tq):
    n_kv = k_ref.shape[0] // kv_tile
    n_sub = qt_ref.shape[1] // tq
    unroll = min(unroll, n_kv)
    scale = scale_ref[0]
    cols = [slice(h * tq, (h + 1) * tq) for h in range(GQA_GROUP)]

    for u in range(n_sub):
        m_ref[...] = jnp.full(m_ref.shape, NEG_BIG, jnp.float32)
        acc_ref[...] = jnp.zeros(acc_ref.shape, jnp.float32)
        l_ref[...] = jnp.zeros(l_ref.shape, jnp.float32)

        def scores(j, u=u):
            s = jnp.concatenate([_scores(k_ref, qt_ref, scale, u, j, h, kv_tile, tq)
                                 for h in range(GQA_GROUP)], axis=1)
            return s, jnp.max(s, axis=0, keepdims=True)

        def trip(jj, carry, scores=scores):
            base = jj * unroll
            s, s_max = scores(base)
            for t in range(unroll):
                if t + 1 < unroll:
                    s_next, s_max_next = scores(base + t + 1)
                m_prev = m_ref[...]
                m_new = jnp.maximum(m_prev, s_max)
                e = jnp.exp2(s - m_new)
                pv = jnp.dot(vt_ref[base + t], e.astype(jnp.bfloat16),
                             preferred_element_type=jnp.float32)
                alpha = jnp.exp2(m_prev - m_new)
                acc_ref[...] = alpha * acc_ref[...] + pv
                l_ref[...] = alpha * l_ref[...] + _sublane_partial_sum(e)
                m_ref[...] = m_new
                if t + 1 < unroll:
                    s, s_max = s_next, s_max_next
            return carry

        lax.fori_loop(0, n_kv // unroll, trip, 0)
        _store_attention([acc_ref[:, c] for c in cols], [l_ref[:, c] for c in cols],
                         o_ref, u, tq)


def _attn_call(score_scale, qt, k, vt, *, bounded, batch, seq_len, tq, q_block, kv_tile):
    n_tok = batch * seq_len
    nq = seq_len // q_block
    kv_blocks = seq_len // kv_tile
    width = GQA_GROUP * tq
    body = _attn_bounded_kernel if bounded else _attn_online_kernel
    scratch = [pltpu.VMEM((HEAD_DIM, width), jnp.float32),
               pltpu.VMEM((SUBLANES, width), jnp.float32)]
    if not bounded:
        scratch.append(pltpu.VMEM((1, width), jnp.float32))
    return pl.pallas_call(
        functools.partial(body, kv_tile=kv_tile, unroll=KV_UNROLL, tq=tq),
        grid=(batch, N_KV_HEADS, nq),
        in_specs=[
            pl.BlockSpec(memory_space=pltpu.SMEM),
            pl.BlockSpec((GQA_GROUP * SPLIT_DEPTH, q_block), lambda b, g, i: (g, b * nq + i)),
            pl.BlockSpec((seq_len, SPLIT_DEPTH), lambda b, g, i: (b, g)),
            pl.BlockSpec((None, kv_blocks, HEAD_DIM, kv_tile), lambda b, g, i: (g, b, 0, 0)),
        ],
        out_specs=pl.BlockSpec((q_block, GQA_GROUP * HEAD_DIM),
                               lambda b, g, i: (b * nq + i, g)),
        out_shape=jax.ShapeDtypeStruct((n_tok, D_MODEL), jnp.bfloat16),
        scratch_shapes=scratch,
        compiler_params=_params(("parallel", "parallel", "arbitrary")),
        name="attn_bounded" if bounded else "attn_online",
    )(score_scale, qt, k, vt)


def _out_kernel(x_ref, attn_ref, sz_ref, sga_ref, cg_ref, wao_ref, wo_ref, g_ref, b_ref,
                y_ref):
    f32 = jnp.float32
    a_in = (attn_ref[...].astype(f32) * sz_ref[...].astype(f32)).astype(jnp.bfloat16)
    a_out = jnp.dot(a_in, wao_ref[...], preferred_element_type=jnp.float32)
    merged = sga_ref[...].astype(f32) * a_out + cg_ref[...]
    out = jnp.dot(merged.astype(jnp.bfloat16), wo_ref[...],
                  preferred_element_type=jnp.float32)
    r = DN_ALPHA * x_ref[...] + out
    mu = jnp.mean(r, axis=-1, keepdims=True)
    d = r - mu
    var = jnp.mean(d * d, axis=-1, keepdims=True)
    y_ref[...] = d * lax.rsqrt(var + LN_EPS) * g_ref[...] + b_ref[...]


def _out_call(x2, attn, sz, sga, cg, w_attn_out, w_o, ln_g, ln_b, *, tm):
    n_tok = x2.shape[0]
    row_spec = pl.BlockSpec((tm, D_MODEL), lambda i: (i, 0))
    return pl.pallas_call(
        _out_kernel,
        grid=(n_tok // tm,),
        in_specs=[row_spec] * 5 + [_full(w_attn_out.shape), _full(w_o.shape),
                                    _full(ln_g.shape), _full(ln_b.shape)],
        out_specs=row_spec,
        out_shape=jax.ShapeDtypeStruct((n_tok, D_MODEL), jnp.float32),
        compiler_params=_params(("parallel",)),
        name="out",
    )(x2, attn, sz, sga, cg, w_attn_out, w_o, ln_g, ln_b)


def _axial_tables_t(seq_len):
    rows = seq_len // GRID_W
    row = jnp.repeat(jnp.arange(rows, dtype=jnp.float32), GRID_W)
    col = jnp.tile(jnp.arange(GRID_W, dtype=jnp.float32), rows)
    inv_freq = ROPE_THETA ** (-jnp.arange(0, ROPE_HALF, 2, dtype=jnp.float32) / ROPE_HALF)
    ang_r = row[:, None] * inv_freq
    ang_c = col[:, None] * inv_freq
    return tuple(t.T for t in (jnp.cos(ang_r), jnp.sin(ang_r), jnp.cos(ang_c), jnp.sin(ang_c)))


def _tile(n, want):
    t = min(n, want)
    assert n % t == 0, (n, t)
    return t


def _layer(x, p):
    batch, seq_len, _ = x.shape
    x2 = x.reshape(batch * seq_len, D_MODEL)
    tm = _tile(seq_len, TOKEN_TILE)
    tq = _tile(seq_len, Q_TILE)
    kv_tile = _tile(seq_len, KV_TILE)
    tabs = _axial_tables_t(seq_len)
    qt, k, vt = _qkv_call(x2, p["w_qkv_t"], p["b_qkv"], p["q_gain"], p["k_gain"], tabs,
                          seq_len=seq_len, tm=max(tm, kv_tile), kv_tile=kv_tile)
    sz, sga, cg = _gates_call(x2, p["w_rest"], p["b_rest"], p["conv_w"], p["conv_b"],
                              p["w_conv_out"], seq_len=seq_len, tm=tm)
    attend = functools.partial(_attn_call, batch=batch, seq_len=seq_len, tq=tq,
                               q_block=_tile(seq_len, Q_BLOCK), kv_tile=kv_tile)
    score_scale = p["score_scale"].reshape(1)
    attn = lax.cond(p["logit_bound"] <= SAFE_LOGIT_BOUND,
                    functools.partial(attend, bounded=True),
                    functools.partial(attend, bounded=False), score_scale, qt, k, vt)
    y = _out_call(x2, attn, sz, sga, cg, p["w_attn_out"], p["w_o"], p["ln_g"], p["ln_b"],
                  tm=tm)
    return y.reshape(batch, seq_len, D_MODEL)


def _prepare(w_in, b_in, q_gain, k_gain, conv_w, conv_b, w_attn_out, w_conv_out, w_o,
             ln_g, ln_b):
    bf = jnp.bfloat16
    tiny = jnp.finfo(jnp.float32).tiny
    q_peak = jnp.maximum(jnp.max(jnp.abs(q_gain)), tiny)
    k_peak = jnp.maximum(jnp.max(jnp.abs(k_gain)), tiny)
    return {
        "w_qkv_t": w_in[:, :QKV_WIDTH].T.astype(bf),
        "b_qkv": b_in[:QKV_WIDTH].reshape(QKV_WIDTH, 1),
        "q_gain": (q_gain * (OPERAND_PEAK / q_peak)).reshape(HEAD_DIM, 1),
        "k_gain": (k_gain * (OPERAND_PEAK / k_peak)).reshape(HEAD_DIM, 1),
        "score_scale": (SCORE_SCALE * LOG2_E / OPERAND_PEAK ** 2) * q_peak * k_peak,
        "logit_bound": math.sqrt(HEAD_DIM) * q_peak * k_peak,
        "w_rest": w_in[:, QKV_WIDTH:].astype(bf),
        "b_rest": b_in[QKV_WIDTH:].reshape(1, -1),
        "conv_w": conv_w,
        "conv_b": conv_b.reshape(1, D_MODEL),
        "w_conv_out": w_conv_out.astype(bf),
        "w_attn_out": w_attn_out.astype(bf),
        "w_o": w_o.astype(bf),
        "ln_g": ln_g.reshape(1, D_MODEL),
        "ln_b": ln_b.reshape(1, D_MODEL),
    }


def kernel(x_prompt, x_sample, w_in, b_in, q_gain, k_gain, conv_w, conv_b, w_attn_out,
           w_conv_out, w_o, ln_g, ln_b):
    depth = w_in.shape[0]
    y_prompt, y_sample = x_prompt, x_sample
    for l in range(depth):
        p = _prepare(w_in[l], b_in[l], q_gain[l], k_gain[l], conv_w[l], conv_b[l],
                     w_attn_out[l], w_conv_out[l], w_o[l], ln_g[l], ln_b[l])
        y_prompt = _layer(y_prompt, p)
        y_sample = _layer(y_sample, p)
    return (y_prompt, y_sample)
```

```python
import functools
import math

import jax
import jax.numpy as jnp
from jax import lax
from jax.experimental import pallas as pl
from jax.experimental.pallas import tpu as pltpu

D_MODEL = 1024
HEAD_DIM = 64
N_HEADS = 16
N_KV_HEADS = 4
GQA_GROUP = N_HEADS // N_KV_HEADS
KV_WIDTH = N_KV_HEADS * HEAD_DIM
QKV_WIDTH = D_MODEL + 2 * KV_WIDTH
GRID_W = 64
ROPE_THETA = 10000.0
ROPE_HALF = HEAD_DIM // 2
ROPE_PAIR = ROPE_HALF // 2
NORM_EPS = 1e-6
LN_EPS = 1e-5
DN_ALPHA = 2.0 ** 0.25
SCORE_SCALE = 1.0 / math.sqrt(HEAD_DIM)
LOG2_E = math.log2(math.e)

SUBLANES = 8
VMEM_LIMIT_BYTES = 56 * 1024 * 1024

FP8 = jnp.float8_e4m3fn
LO_SCALE = 16.0
SPLIT_DEPTH = 4 * HEAD_DIM
OPERAND_PEAK = 16.0
NEG_BIG = -1e30
SAFE_LOGIT_BOUND = 30.0

TOKEN_TILE = 512
HALO = SUBLANES
Q_TILE = 256
Q_BLOCK = 2048
KV_TILE = 512
KV_UNROLL = 16


def _params(sem):
    return pltpu.CompilerParams(dimension_semantics=sem,
                                vmem_limit_bytes=VMEM_LIMIT_BYTES)


def _full(shape):
    return pl.BlockSpec(shape, lambda *_: (0,) * len(shape))


def _norm_rope_t(u, gain, cr, sr, cc, sc):
    ms = jnp.mean(u * u, axis=0, keepdims=True)
    u = u * lax.rsqrt(ms + NORM_EPS) * gain
    p = ROPE_PAIR
    r1, r2, c1, c2 = u[0:p], u[p:2 * p], u[2 * p:3 * p], u[3 * p:4 * p]
    return jnp.concatenate([r1 * cr - r2 * sr, r2 * cr + r1 * sr,
                            c1 * cc - c2 * sc, c2 * cc + c1 * sc], axis=0)


def _split_fp8(u):
    hi = u.astype(FP8)
    lo = ((u - hi.astype(jnp.float32)) * LO_SCALE).astype(FP8)
    return hi, lo


def _qkv_kernel(x_ref, w_ref, b_ref, qg_ref, kg_ref, cr_ref, sr_ref, cc_ref, sc_ref,
                qt_ref, k_ref, vt_ref, *, kv_tile, chunk):
    tm = x_ref.shape[0]
    qg, kg = qg_ref[...], kg_ref[...]
    d, s = HEAD_DIM, 1.0 / LO_SCALE
    for c in range(tm // chunk):
        tok = slice(c * chunk, (c + 1) * chunk)
        xb = x_ref[tok, :].astype(jnp.bfloat16)
        pt = lax.dot_general(w_ref[...], xb, (((1,), (1,)), ((), ())),
                             preferred_element_type=jnp.float32) + b_ref[...]
        cr, sr, cc, sc = cr_ref[:, tok], sr_ref[:, tok], cc_ref[:, tok], sc_ref[:, tok]
        for h in range(N_HEADS):
            hi, lo = _split_fp8(_norm_rope_t(pt[h * d:(h + 1) * d], qg, cr, sr, cc, sc))
            for part, val in enumerate((hi, hi, lo, lo)):
                row = h * SPLIT_DEPTH + part * d
                qt_ref[c, row:row + d, :] = val
        kparts = []
        for g in range(N_KV_HEADS):
            u = pt[D_MODEL + g * d:D_MODEL + (g + 1) * d]
            hi, lo = _split_fp8(_norm_rope_t(u, kg, cr, sr, cc, sc))
            hi, lo = hi.astype(jnp.float32), lo.astype(jnp.float32)
            kparts += [hi, lo * s, hi * s, lo * (s * s)]
        kt = jnp.concatenate(kparts, axis=0)
        k_ref[tok, :] = kt.T.astype(FP8)
        for g in range(N_KV_HEADS):
            v = pt[D_MODEL + KV_WIDTH + g * d:D_MODEL + KV_WIDTH + (g + 1) * d]
            tile, off = divmod(c * chunk, kv_tile)
            vt_ref[g, tile, :, off:off + chunk] = v.astype(jnp.bfloat16)


def _qkv_call(x2, w_qkv_t, b_qkv, qg, kg, tabs, *, seq_len, tm, kv_tile, chunk):
    n_tok = x2.shape[0]
    tiles_per_seq = seq_len // tm
    tab_spec = pl.BlockSpec((ROPE_PAIR, tm), lambda i: (0, i % tiles_per_seq))
    return pl.pallas_call(
        functools.partial(_qkv_kernel, kv_tile=kv_tile, chunk=chunk),
        grid=(n_tok // tm,),
        in_specs=[
            pl.BlockSpec((tm, D_MODEL), lambda i: (i, 0)),
            _full((QKV_WIDTH, D_MODEL)),
            _full((QKV_WIDTH, 1)),
            _full((HEAD_DIM, 1)),
            _full((HEAD_DIM, 1)),
            tab_spec, tab_spec, tab_spec, tab_spec,
        ],
        out_specs=[
            pl.BlockSpec((tm // chunk, N_HEADS * SPLIT_DEPTH, chunk), lambda i: (i, 0, 0)),
            pl.BlockSpec((tm, N_KV_HEADS * SPLIT_DEPTH), lambda i: (i, 0)),
            pl.BlockSpec((N_KV_HEADS, tm // kv_tile, HEAD_DIM, kv_tile),
                         lambda i: (0, i, 0, 0)),
        ],
        out_shape=[
            jax.ShapeDtypeStruct((n_tok // chunk, N_HEADS * SPLIT_DEPTH, chunk), FP8),
            jax.ShapeDtypeStruct((n_tok, N_KV_HEADS * SPLIT_DEPTH), FP8),
            jax.ShapeDtypeStruct((N_KV_HEADS, n_tok // kv_tile, HEAD_DIM, kv_tile),
                                 jnp.bfloat16),
        ],
        compiler_params=_params(("parallel",)),
        name="qkv",
    )(x2, w_qkv_t, b_qkv, qg, kg, *tabs)


def _silu(x):
    return x * jax.nn.sigmoid(x)


def _gates_kernel(x_ref, xp_ref, xn_ref, w_ref, b_ref, cw_ref, cb_ref, wco_ref,
                  sz_ref, sga_ref, cg_ref, *, tiles_per_seq):
    tm = x_ref.shape[0]
    i = pl.program_id(0)
    x = x_ref[...]
    xb = x.astype(jnp.bfloat16)
    xcat = jnp.concatenate([xp_ref[...], x, xn_ref[...]], axis=0).astype(jnp.bfloat16)

    def proj(lhs, c):
        sl = slice(c * D_MODEL, (c + 1) * D_MODEL)
        return jnp.dot(lhs, w_ref[:, sl], preferred_element_type=jnp.float32) + b_ref[:, sl]

    sz_ref[...] = _silu(proj(xb, 0)).astype(sz_ref.dtype)
    sga_ref[...] = jax.nn.sigmoid(proj(xb, 5)).astype(sga_ref.dtype)

    u = proj(xcat, 2) * proj(xcat, 3)
    rows = lax.broadcasted_iota(jnp.int32, u.shape, 0)
    first = (i % tiles_per_seq) == 0
    last = (i % tiles_per_seq) == tiles_per_seq - 1
    outside = ((rows < HALO) & first) | ((rows >= tm + HALO) & last)
    u = jnp.where(outside, 0.0, u)
    n = tm + 2 * HALO
    u_prev = pltpu.roll(u, 1, axis=0)[HALO:HALO + tm]
    u_next = pltpu.roll(u, n - 1, axis=0)[HALO:HALO + tm]
    cw = cw_ref[...]
    conv = (u_prev * cw[0:1] + u[HALO:HALO + tm] * cw[1:2] + u_next * cw[2:3]
            + cb_ref[...])
    act = proj(xb, 1) * conv * _silu(proj(xb, 4))
    c_out = jnp.dot(act.astype(jnp.bfloat16), wco_ref[...],
                    preferred_element_type=jnp.float32)
    cg_ref[...] = jax.nn.sigmoid(proj(xb, 6)) * c_out


def _gates_call(x2, w_rest, b_rest, conv_w, conv_b, w_conv_out, *, seq_len, tm):
    n_tok = x2.shape[0]
    halo_blocks = tm // HALO
    last_halo = n_tok // HALO - 1
    row_spec = pl.BlockSpec((tm, D_MODEL), lambda i: (i, 0))
    gate = jax.ShapeDtypeStruct((n_tok, D_MODEL), jnp.bfloat16)
    branch = jax.ShapeDtypeStruct((n_tok, D_MODEL), jnp.float32)
    return pl.pallas_call(
        functools.partial(_gates_kernel, tiles_per_seq=seq_len // tm),
        grid=(n_tok // tm,),
        in_specs=[
            row_spec,
            pl.BlockSpec((HALO, D_MODEL),
                         lambda i: (jnp.maximum(i * halo_blocks - 1, 0), 0)),
            pl.BlockSpec((HALO, D_MODEL),
                         lambda i: (jnp.minimum((i + 1) * halo_blocks, last_halo), 0)),
            _full(w_rest.shape),
            _full(b_rest.shape),
            _full(conv_w.shape),
            _full(conv_b.shape),
            _full(w_conv_out.shape),
        ],
        out_specs=[row_spec, row_spec, row_spec],
        out_shape=[gate, gate, branch],
        compiler_params=_params(("parallel",)),
        name="gates",
    )(x2, x2, x2, w_rest, b_rest, conv_w, conv_b, w_conv_out)


def _scores(k_ref, qt_ref, scale, u, j, h, kv_tile):
    start = pl.multiple_of(j * kv_tile, kv_tile)
    s = jnp.dot(k_ref[pl.ds(start, kv_tile), :],
                qt_ref[u, h * SPLIT_DEPTH:(h + 1) * SPLIT_DEPTH, :],
                preferred_element_type=jnp.float32)
    return s * scale


def _sublane_partial_sum(e):
    return jnp.sum(e.reshape(e.shape[0] // SUBLANES, SUBLANES, e.shape[1]), axis=0)


def _store_attention(acc, row_sum, o_ref, u):
    tq = acc[0].shape[1]
    o = [a / jnp.sum(l, axis=0, keepdims=True) for a, l in zip(acc, row_sum)]
    rows = pl.ds(pl.multiple_of(u * tq, tq), tq)
    o_ref[rows, :] = jnp.concatenate(o, axis=0).T.astype(o_ref.dtype)


def _attn_bounded_kernel(scale_ref, qt_ref, k_ref, vt_ref, o_ref, p_ref, part_ref, acc_ref,
                         l_ref, *, kv_tile, pass_items):
    n_sub, _, tq = qt_ref.shape
    n_kv = k_ref.shape[0] // kv_tile
    scale = scale_ref[0]
    heads = range(GQA_GROUP)
    pass_items = min(pass_items, n_sub * n_kv)
    if n_kv >= pass_items:
        passes = n_kv // pass_items
        n_pass = n_sub * passes

        def item(t, i):
            return t // passes, (t % passes) * pass_items + i

        def starts_sub_tile(t, i):
            return (t % passes == 0) if i == 0 else False
    else:
        per_pass = pass_items // n_kv
        n_pass = n_sub // per_pass

        def item(t, i):
            return t * per_pass + i // n_kv, i % n_kv

        def starts_sub_tile(t, i):
            return i % n_kv == 0

    def probs(u, j, h):
        e = jnp.exp2(_scores(k_ref, qt_ref, scale, u, j, h, kv_tile))
        return e.astype(jnp.bfloat16), _sublane_partial_sum(e)

    for h in heads:
        p_ref[h], part_ref[h] = probs(0, 0, h)
        acc_ref[h] = jnp.zeros(acc_ref.shape[1:], jnp.float32)
        l_ref[h] = jnp.ones(l_ref.shape[1:], jnp.float32)

    def one_pass(t, carry):
        t_next = jnp.minimum(t + 1, n_pass - 1)
        acc = [acc_ref[h] for h in heads]
        row_sum = [l_ref[h] for h in heads]
        u_prev, _ = item(jnp.maximum(t - 1, 0), pass_items - 1)
        _store_attention(acc, row_sum, o_ref, u_prev)
        fresh = starts_sub_tile(t, 0)
        acc = [jnp.where(fresh, 0.0, a) for a in acc]
        row_sum = [jnp.where(fresh, 0.0, l) + part_ref[h] for h, l in zip(heads, row_sum)]
        p = [p_ref[h] for h in heads]
        for i in range(pass_items):
            u, j = item(t, i)
            last = i + 1 == pass_items
            u_ahead, j_ahead = item(t_next, 0) if last else item(t, i + 1)
            switch = (not last) and starts_sub_tile(t, i + 1)
            p_next, row_sum_next = [], []
            for h in heads:
                p_h, part = probs(u_ahead, j_ahead, h)
                p_next.append(p_h)
                if last:
                    part_ref[h] = part
                elif switch:
                    row_sum_next.append(part)
                else:
                    row_sum[h] = row_sum[h] + part
                acc[h] = acc[h] + jnp.dot(vt_ref[j], p[h], preferred_element_type=jnp.float32)
            p = p_next
            if switch:
                _store_attention(acc, row_sum, o_ref, u)
                acc = [jnp.zeros_like(a) for a in acc]
                row_sum = row_sum_next
        for h in heads:
            p_ref[h] = p[h]
            acc_ref[h] = acc[h]
            l_ref[h] = row_sum[h]
        return carry

    lax.fori_loop(0, n_pass, one_pass, 0)
    _store_attention([acc_ref[h] for h in heads], [l_ref[h] for h in heads], o_ref, n_sub - 1)


def _attn_online_kernel(scale_ref, qt_ref, k_ref, vt_ref, o_ref, acc_ref, l_ref, m_ref, *,
                        kv_tile):
    n_sub, _, tq = qt_ref.shape
    n_kv = k_ref.shape[0] // kv_tile
    unroll = min(KV_UNROLL, n_kv)
    scale = scale_ref[0]
    cols = [slice(h * tq, (h + 1) * tq) for h in range(GQA_GROUP)]

    def sub_tile(u, carry):
        m_ref[...] = jnp.full(m_ref.shape, NEG_BIG, jnp.float32)
        acc_ref[...] = jnp.zeros(acc_ref.shape, jnp.float32)
        l_ref[...] = jnp.zeros(l_ref.shape, jnp.float32)

        def scores(j):
            s = jnp.concatenate([_scores(k_ref, qt_ref, scale, u, j, h, kv_tile)
                                 for h in range(GQA_GROUP)], axis=1)
            return s, jnp.max(s, axis=0, keepdims=True)

        def trip(jj, inner):
            base = jj * unroll
            s, s_max = scores(base)
            for t in range(unroll):
                if t + 1 < unroll:
                    s_next, s_max_next = scores(base + t + 1)
                m_prev = m_ref[...]
                m_new = jnp.maximum(m_prev, s_max)
                e = jnp.exp2(s - m_new)
                pv = jnp.dot(vt_ref[base + t], e.astype(jnp.bfloat16),
                             preferred_element_type=jnp.float32)
                alpha = jnp.exp2(m_prev - m_new)
                acc_ref[...] = alpha * acc_ref[...] + pv
                l_ref[...] = alpha * l_ref[...] + _sublane_partial_sum(e)
                m_ref[...] = m_new
                if t + 1 < unroll:
                    s, s_max = s_next, s_max_next
            return inner

        lax.fori_loop(0, n_kv // unroll, trip, 0)
        _store_attention([acc_ref[:, c] for c in cols], [l_ref[:, c] for c in cols], o_ref, u)
        return carry

    lax.fori_loop(0, n_sub, sub_tile, 0)


def _attn_call(score_scale, qt, k, vt, *, bounded, batch, seq_len, tq, q_block, kv_tile):
    n_tok = batch * seq_len
    nq = seq_len // q_block
    n_sub = q_block // tq
    kv_blocks = seq_len // kv_tile
    width = GQA_GROUP * tq
    if bounded:
        body = functools.partial(_attn_bounded_kernel, kv_tile=kv_tile,
                                 pass_items=KV_UNROLL)
        scratch = [pltpu.VMEM((GQA_GROUP, kv_tile, tq), jnp.bfloat16),
                   pltpu.VMEM((GQA_GROUP, SUBLANES, tq), jnp.float32),
                   pltpu.VMEM((GQA_GROUP, HEAD_DIM, tq), jnp.float32),
                   pltpu.VMEM((GQA_GROUP, SUBLANES, tq), jnp.float32)]
    else:
        body = functools.partial(_attn_online_kernel, kv_tile=kv_tile)
        scratch = [pltpu.VMEM((HEAD_DIM, width), jnp.float32),
                   pltpu.VMEM((SUBLANES, width), jnp.float32),
                   pltpu.VMEM((1, width), jnp.float32)]
    return pl.pallas_call(
        body,
        grid=(batch, N_KV_HEADS, nq),
        in_specs=[
            pl.BlockSpec(memory_space=pltpu.SMEM),
            pl.BlockSpec((n_sub, GQA_GROUP * SPLIT_DEPTH, tq),
                         lambda b, g, i: (b * nq + i, g, 0)),
            pl.BlockSpec((seq_len, SPLIT_DEPTH), lambda b, g, i: (b, g)),
            pl.BlockSpec((None, kv_blocks, HEAD_DIM, kv_tile), lambda b, g, i: (g, b, 0, 0)),
        ],
        out_specs=pl.BlockSpec((q_block, GQA_GROUP * HEAD_DIM),
                               lambda b, g, i: (b * nq + i, g)),
        out_shape=jax.ShapeDtypeStruct((n_tok, D_MODEL), jnp.bfloat16),
        scratch_shapes=scratch,
        compiler_params=_params(("parallel", "parallel", "arbitrary")),
        name="attn_bounded" if bounded else "attn_online",
    )(score_scale, qt, k, vt)


def _out_kernel(x_ref, attn_ref, sz_ref, sga_ref, cg_ref, wao_ref, wo_ref, g_ref, b_ref,
                y_ref):
    f32 = jnp.float32
    a_in = (attn_ref[...].astype(f32) * sz_ref[...].astype(f32)).astype(jnp.bfloat16)
    a_out = jnp.dot(a_in, wao_ref[...], preferred_element_type=jnp.float32)
    merged = sga_ref[...].astype(f32) * a_out + cg_ref[...]
    out = jnp.dot(merged.astype(jnp.bfloat16), wo_ref[...],
                  preferred_element_type=jnp.float32)
    r = DN_ALPHA * x_ref[...] + out
    mu = jnp.mean(r, axis=-1, keepdims=True)
    d = r - mu
    var = jnp.mean(d * d, axis=-1, keepdims=True)
    y_ref[...] = d * lax.rsqrt(var + LN_EPS) * g_ref[...] + b_ref[...]


def _out_call(x2, attn, sz, sga, cg, w_attn_out, w_o, ln_g, ln_b, *, tm):
    n_tok = x2.shape[0]
    row_spec = pl.BlockSpec((tm, D_MODEL), lambda i: (i, 0))
    return pl.pallas_call(
        _out_kernel,
        grid=(n_tok // tm,),
        in_specs=[row_spec] * 5 + [_full(w_attn_out.shape), _full(w_o.shape),
                                    _full(ln_g.shape), _full(ln_b.shape)],
        out_specs=row_spec,
        out_shape=jax.ShapeDtypeStruct((n_tok, D_MODEL), jnp.float32),
        compiler_params=_params(("parallel",)),
        name="out",
    )(x2, attn, sz, sga, cg, w_attn_out, w_o, ln_g, ln_b)


def _axial_tables_t(seq_len):
    rows = seq_len // GRID_W
    row = jnp.repeat(jnp.arange(rows, dtype=jnp.float32), GRID_W)
    col = jnp.tile(jnp.arange(GRID_W, dtype=jnp.float32), rows)
    inv_freq = ROPE_THETA ** (-jnp.arange(0, ROPE_HALF, 2, dtype=jnp.float32) / ROPE_HALF)
    ang_r = row[:, None] * inv_freq
    ang_c = col[:, None] * inv_freq
    return tuple(t.T for t in (jnp.cos(ang_r), jnp.sin(ang_r), jnp.cos(ang_c), jnp.sin(ang_c)))


def _tile(n, want):
    t = min(n, want)
    assert n % t == 0, (n, t)
    return t


def _layer(x, p):
    batch, seq_len, _ = x.shape
    x2 = x.reshape(batch * seq_len, D_MODEL)
    tm = _tile(seq_len, TOKEN_TILE)
    tq = _tile(seq_len, Q_TILE)
    kv_tile = _tile(seq_len, KV_TILE)
    tabs = _axial_tables_t(seq_len)
    qt, k, vt = _qkv_call(x2, p["w_qkv_t"], p["b_qkv"], p["q_gain"], p["k_gain"], tabs,
                          seq_len=seq_len, tm=max(tm, kv_tile), kv_tile=kv_tile, chunk=tq)
    sz, sga, cg = _gates_call(x2, p["w_rest"], p["b_rest"], p["conv_w"], p["conv_b"],
                              p["w_conv_out"], seq_len=seq_len, tm=tm)
    attend = functools.partial(_attn_call, batch=batch, seq_len=seq_len, tq=tq,
                               q_block=_tile(seq_len, Q_BLOCK), kv_tile=kv_tile)
    score_scale = p["score_scale"].reshape(1)
    attn = lax.cond(p["logit_bound"] <= SAFE_LOGIT_BOUND,
                    functools.partial(attend, bounded=True),
                    functools.partial(attend, bounded=False), score_scale, qt, k, vt)
    y = _out_call(x2, attn, sz, sga, cg, p["w_attn_out"], p["w_o"], p["ln_g"], p["ln_b"],
                  tm=tm)
    return y.reshape(batch, seq_len, D_MODEL)


def _prepare(w_in, b_in, q_gain, k_gain, conv_w, conv_b, w_attn_out, w_conv_out, w_o,
             ln_g, ln_b):
    bf = jnp.bfloat16
    tiny = jnp.finfo(jnp.float32).tiny
    q_peak = jnp.maximum(jnp.max(jnp.abs(q_gain)), tiny)
    k_peak = jnp.maximum(jnp.max(jnp.abs(k_gain)), tiny)
    return {
        "w_qkv_t": w_in[:, :QKV_WIDTH].T.astype(bf),
        "b_qkv": b_in[:QKV_WIDTH].reshape(QKV_WIDTH, 1),
        "q_gain": (q_gain * (OPERAND_PEAK / q_peak)).reshape(HEAD_DIM, 1),
        "k_gain": (k_gain * (OPERAND_PEAK / k_peak)).reshape(HEAD_DIM, 1),
        "score_scale": (SCORE_SCALE * LOG2_E / OPERAND_PEAK ** 2) * q_peak * k_peak,
        "logit_bound": math.sqrt(HEAD_DIM) * q_peak * k_peak,
        "w_rest": w_in[:, QKV_WIDTH:].astype(bf),
        "b_rest": b_in[QKV_WIDTH:].reshape(1, -1),
        "conv_w": conv_w,
        "conv_b": conv_b.reshape(1, D_MODEL),
        "w_conv_out": w_conv_out.astype(bf),
        "w_attn_out": w_attn_out.astype(bf),
        "w_o": w_o.astype(bf),
        "ln_g": ln_g.reshape(1, D_MODEL),
        "ln_b": ln_b.reshape(1, D_MODEL),
    }


def kernel(x_prompt, x_sample, w_in, b_in, q_gain, k_gain, conv_w, conv_b, w_attn_out,
           w_conv_out, w_o, ln_g, ln_b):
    depth = w_in.shape[0]
    y_prompt, y_sample = x_prompt, x_sample
    for l in range(depth):
        p = _prepare(w_in[l], b_in[l], q_gain[l], k_gain[l], conv_w[l], conv_b[l],
                     w_attn_out[l], w_conv_out[l], w_o[l], ln_g[l], ln_b[l])
        y_prompt = _layer(y_prompt, p)
        y_sample = _layer(y_sample, p)
    return (y_prompt, y_sample)
```

```python
import functools
import math

import jax
import jax.numpy as jnp
from jax import lax
from jax.experimental import pallas as pl
from jax.experimental.pallas import tpu as pltpu

D_MODEL = 1024
HEAD_DIM = 64
N_HEADS = 16
N_KV_HEADS = 4
GQA_GROUP = N_HEADS // N_KV_HEADS
KV_WIDTH = N_KV_HEADS * HEAD_DIM
QKV_WIDTH = D_MODEL + 2 * KV_WIDTH
GRID_W = 64
ROPE_THETA = 10000.0
ROPE_HALF = HEAD_DIM // 2
ROPE_PAIR = ROPE_HALF // 2
NORM_EPS = 1e-6
LN_EPS = 1e-5
DN_ALPHA = 2.0 ** 0.25
SCORE_SCALE = 1.0 / math.sqrt(HEAD_DIM)
LOG2_E = math.log2(math.e)

SUBLANES = 8
VMEM_LIMIT_BYTES = 56 * 1024 * 1024

FP8 = jnp.float8_e4m3fn
LO_SCALE = 16.0
SPLIT_DEPTH = 4 * HEAD_DIM
OPERAND_PEAK = 16.0
NEG_BIG = -1e30
SAFE_LOGIT_BOUND = 30.0

TOKEN_TILE = 512
HALO = SUBLANES
Q_TILE = 256
Q_BLOCK = 2048
KV_TILE = 512
KV_UNROLL = 16


def _params(sem):
    return pltpu.CompilerParams(dimension_semantics=sem,
                                vmem_limit_bytes=VMEM_LIMIT_BYTES)


def _full(shape):
    return pl.BlockSpec(shape, lambda *_: (0,) * len(shape))


def _norm_rope_t(u, gain, cr, sr, cc, sc):
    ms = jnp.mean(u * u, axis=0, keepdims=True)
    u = u * lax.rsqrt(ms + NORM_EPS) * gain
    p = ROPE_PAIR
    r1, r2, c1, c2 = u[0:p], u[p:2 * p], u[2 * p:3 * p], u[3 * p:4 * p]
    return jnp.concatenate([r1 * cr - r2 * sr, r2 * cr + r1 * sr,
                            c1 * cc - c2 * sc, c2 * cc + c1 * sc], axis=0)


def _split_fp8(u):
    hi = u.astype(FP8)
    lo = ((u - hi.astype(jnp.float32)) * LO_SCALE).astype(FP8)
    return hi, lo


def _qkv_kernel(x_ref, w_ref, b_ref, qg_ref, kg_ref, cr_ref, sr_ref, cc_ref, sc_ref,
                qt_ref, k_ref, vt_ref, *, kv_tile, chunk):
    tm = x_ref.shape[0]
    qg, kg = qg_ref[...], kg_ref[...]
    d, s = HEAD_DIM, 1.0 / LO_SCALE
    for c in range(tm // chunk):
        tok = slice(c * chunk, (c + 1) * chunk)
        xb = x_ref[tok, :].astype(jnp.bfloat16)
        pt = lax.dot_general(w_ref[...], xb, (((1,), (1,)), ((), ())),
                             preferred_element_type=jnp.float32) + b_ref[...]
        cr, sr, cc, sc = cr_ref[:, tok], sr_ref[:, tok], cc_ref[:, tok], sc_ref[:, tok]
        for h in range(N_HEADS):
            hi, lo = _split_fp8(_norm_rope_t(pt[h * d:(h + 1) * d], qg, cr, sr, cc, sc))
            for part, val in enumerate((hi, hi, lo, lo)):
                row = h * SPLIT_DEPTH + part * d
                qt_ref[c, row:row + d, :] = val
        kparts = []
        for g in range(N_KV_HEADS):
            u = pt[D_MODEL + g * d:D_MODEL + (g + 1) * d]
            hi, lo = _split_fp8(_norm_rope_t(u, kg, cr, sr, cc, sc))
            hi, lo = hi.astype(jnp.float32), lo.astype(jnp.float32)
            kparts += [hi, lo * s, hi * s, lo * (s * s)]
        kt = jnp.concatenate(kparts, axis=0)
        k_ref[tok, :] = kt.T.astype(FP8)
        for g in range(N_KV_HEADS):
            v = pt[D_MODEL + KV_WIDTH + g * d:D_MODEL + KV_WIDTH + (g + 1) * d]
            tile, off = divmod(c * chunk, kv_tile)
            vt_ref[g, tile, :, off:off + chunk] = v.astype(jnp.bfloat16)


def _qkv_call(x2, w_qkv_t, b_qkv, qg, kg, tabs, *, seq_len, tm, kv_tile, chunk):
    n_tok = x2.shape[0]
    tiles_per_seq = seq_len // tm
    tab_spec = pl.BlockSpec((ROPE_PAIR, tm), lambda i: (0, i % tiles_per_seq))
    return pl.pallas_call(
        functools.partial(_qkv_kernel, kv_tile=kv_tile, chunk=chunk),
        grid=(n_tok // tm,),
        in_specs=[
            pl.BlockSpec((tm, D_MODEL), lambda i: (i, 0)),
            _full((QKV_WIDTH, D_MODEL)),
            _full((QKV_WIDTH, 1)),
            _full((HEAD_DIM, 1)),
            _full((HEAD_DIM, 1)),
            tab_spec, tab_spec, tab_spec, tab_spec,
        ],
        out_specs=[
            pl.BlockSpec((tm // chunk, N_HEADS * SPLIT_DEPTH, chunk), lambda i: (i, 0, 0)),
            pl.BlockSpec((tm, N_KV_HEADS * SPLIT_DEPTH), lambda i: (i, 0)),
            pl.BlockSpec((N_KV_HEADS, tm // kv_tile, HEAD_DIM, kv_tile),
                         lambda i: (0, i, 0, 0)),
        ],
        out_shape=[
            jax.ShapeDtypeStruct((n_tok // chunk, N_HEADS * SPLIT_DEPTH, chunk), FP8),
            jax.ShapeDtypeStruct((n_tok, N_KV_HEADS * SPLIT_DEPTH), FP8),
            jax.ShapeDtypeStruct((N_KV_HEADS, n_tok // kv_tile, HEAD_DIM, kv_tile),
                                 jnp.bfloat16),
        ],
        compiler_params=_params(("parallel",)),
        name="qkv",
    )(x2, w_qkv_t, b_qkv, qg, kg, *tabs)


def _silu(x):
    return x * jax.nn.sigmoid(x)


def _gates_kernel(x_ref, xp_ref, xn_ref, w_ref, b_ref, cw_ref, cb_ref, wco_ref,
                  sz_ref, sga_ref, cg_ref, *, tiles_per_seq):
    tm = x_ref.shape[0]
    i = pl.program_id(0)
    x = x_ref[...]
    xb = x.astype(jnp.bfloat16)
    xcat = jnp.concatenate([xp_ref[...], x, xn_ref[...]], axis=0).astype(jnp.bfloat16)

    def proj(lhs, c):
        sl = slice(c * D_MODEL, (c + 1) * D_MODEL)
        return jnp.dot(lhs, w_ref[:, sl], preferred_element_type=jnp.float32) + b_ref[:, sl]

    sz_ref[...] = _silu(proj(xb, 0)).astype(sz_ref.dtype)
    sga_ref[...] = jax.nn.sigmoid(proj(xb, 5)).astype(sga_ref.dtype)

    u = proj(xcat, 2) * proj(xcat, 3)
    rows = lax.broadcasted_iota(jnp.int32, u.shape, 0)
    first = (i % tiles_per_seq) == 0
    last = (i % tiles_per_seq) == tiles_per_seq - 1
    outside = ((rows < HALO) & first) | ((rows >= tm + HALO) & last)
    u = jnp.where(outside, 0.0, u)
    n = tm + 2 * HALO
    u_prev = pltpu.roll(u, 1, axis=0)[HALO:HALO + tm]
    u_next = pltpu.roll(u, n - 1, axis=0)[HALO:HALO + tm]
    cw = cw_ref[...]
    conv = (u_prev * cw[0:1] + u[HALO:HALO + tm] * cw[1:2] + u_next * cw[2:3]
            + cb_ref[...])
    act = proj(xb, 1) * conv * _silu(proj(xb, 4))
    c_out = jnp.dot(act.astype(jnp.bfloat16), wco_ref[...],
                    preferred_element_type=jnp.float32)
    cg_ref[...] = jax.nn.sigmoid(proj(xb, 6)) * c_out


def _gates_call(x2, w_rest, b_rest, conv_w, conv_b, w_conv_out, *, seq_len, tm):
    n_tok = x2.shape[0]
    halo_blocks = tm // HALO
    last_halo = n_tok // HALO - 1
    row_spec = pl.BlockSpec((tm, D_MODEL), lambda i: (i, 0))
    gate = jax.ShapeDtypeStruct((n_tok, D_MODEL), jnp.bfloat16)
    branch = jax.ShapeDtypeStruct((n_tok, D_MODEL), jnp.float32)
    return pl.pallas_call(
        functools.partial(_gates_kernel, tiles_per_seq=seq_len // tm),
        grid=(n_tok // tm,),
        in_specs=[
            row_spec,
            pl.BlockSpec((HALO, D_MODEL),
                         lambda i: (jnp.maximum(i * halo_blocks - 1, 0), 0)),
            pl.BlockSpec((HALO, D_MODEL),
                         lambda i: (jnp.minimum((i + 1) * halo_blocks, last_halo), 0)),
            _full(w_rest.shape),
            _full(b_rest.shape),
            _full(conv_w.shape),
            _full(conv_b.shape),
            _full(w_conv_out.shape),
        ],
        out_specs=[row_spec, row_spec, row_spec],
        out_shape=[gate, gate, branch],
        compiler_params=_params(("parallel",)),
        name="gates",
    )(x2, x2, x2, w_rest, b_rest, conv_w, conv_b, w_conv_out)


def _scores(k_ref, qt_ref, scale, u, j, h, kv_tile):
    start = pl.multiple_of(j * kv_tile, kv_tile)
    s = jnp.dot(k_ref[pl.ds(start, kv_tile), :],
                qt_ref[u, h * SPLIT_DEPTH:(h + 1) * SPLIT_DEPTH, :],
                preferred_element_type=jnp.float32)
    return s * scale


def _sublane_partial_sum(e):
    return jnp.sum(e.reshape(e.shape[0] // SUBLANES, SUBLANES, e.shape[1]), axis=0)


def _store_attention(acc, row_sum, o_ref, u):
    tq = acc[0].shape[1]
    o = [a / jnp.sum(l, axis=0, keepdims=True) for a, l in zip(acc, row_sum)]
    rows = pl.ds(pl.multiple_of(u * tq, tq), tq)
    o_ref[rows, :] = jnp.concatenate(o, axis=0).T.astype(o_ref.dtype)


def _attn_bounded_kernel(scale_ref, qt_ref, k_ref, vt_ref, o_ref, p_ref, part_ref, acc_ref,
                         l_ref, *, kv_tile, pass_items):
    n_sub, _, tq = qt_ref.shape
    n_kv = k_ref.shape[0] // kv_tile
    scale = scale_ref[0]
    heads = range(GQA_GROUP)
    pass_items = min(pass_items, n_sub * n_kv)
    if n_kv >= pass_items:
        passes = n_kv // pass_items
        n_pass = n_sub * passes

        def item(t, i):
            return t // passes, (t % passes) * pass_items + i

        def starts_sub_tile(t, i):
            return (t >= 0 if passes == 1 else t % passes == 0) if i == 0 else False
    else:
        per_pass = pass_items // n_kv
        n_pass = n_sub // per_pass

        def item(t, i):
            return t * per_pass + i // n_kv, i % n_kv

        def starts_sub_tile(t, i):
            return i % n_kv == 0

    def probs(u, j, h):
        e = jnp.exp2(_scores(k_ref, qt_ref, scale, u, j, h, kv_tile))
        return e.astype(jnp.bfloat16), _sublane_partial_sum(e)

    for h in heads:
        p_ref[h], part_ref[h] = probs(0, 0, h)
        acc_ref[h] = jnp.zeros(acc_ref.shape[1:], jnp.float32)
        l_ref[h] = jnp.ones(l_ref.shape[1:], jnp.float32)

    def one_pass(t, carry):
        t_next = jnp.minimum(t + 1, n_pass - 1)
        acc = [acc_ref[h] for h in heads]
        row_sum = [l_ref[h] for h in heads]
        u_prev, _ = item(jnp.maximum(t - 1, 0), pass_items - 1)
        _store_attention(acc, row_sum, o_ref, u_prev)
        fresh = starts_sub_tile(t, 0)
        acc = [jnp.where(fresh, 0.0, a) for a in acc]
        row_sum = [jnp.where(fresh, 0.0, l) + part_ref[h] for h, l in zip(heads, row_sum)]
        p = [p_ref[h] for h in heads]
        for i in range(pass_items):
            u, j = item(t, i)
            last = i + 1 == pass_items
            u_ahead, j_ahead = item(t_next, 0) if last else item(t, i + 1)
            switch = (not last) and starts_sub_tile(t, i + 1)
            p_next, row_sum_next = [], []
            for h in heads:
                p_h, part = probs(u_ahead, j_ahead, h)
                p_next.append(p_h)
                if last:
                    part_ref[h] = part
                elif switch:
                    row_sum_next.append(part)
                else:
                    row_sum[h] = row_sum[h] + part
                acc[h] = acc[h] + jnp.dot(vt_ref[j], p[h], preferred_element_type=jnp.float32)
            p = p_next
            if switch:
                _store_attention(acc, row_sum, o_ref, u)
                acc = [jnp.zeros_like(a) for a in acc]
                row_sum = row_sum_next
        for h in heads:
            p_ref[h] = p[h]
            acc_ref[h] = acc[h]
            l_ref[h] = row_sum[h]
        return carry

    lax.fori_loop(0, n_pass, one_pass, 0)
    _store_attention([acc_ref[h] for h in heads], [l_ref[h] for h in heads], o_ref, n_sub - 1)


def _attn_online_kernel(scale_ref, qt_ref, k_ref, vt_ref, o_ref, acc_ref, l_ref, m_ref, *,
                        kv_tile):
    n_sub, _, tq = qt_ref.shape
    n_kv = k_ref.shape[0] // kv_tile
    unroll = min(KV_UNROLL, n_kv)
    scale = scale_ref[0]
    cols = [slice(h * tq, (h + 1) * tq) for h in range(GQA_GROUP)]

    def sub_tile(u, carry):
        m_ref[...] = jnp.full(m_ref.shape, NEG_BIG, jnp.float32)
        acc_ref[...] = jnp.zeros(acc_ref.shape, jnp.float32)
        l_ref[...] = jnp.zeros(l_ref.shape, jnp.float32)

        def scores(j):
            s = jnp.concatenate([_scores(k_ref, qt_ref, scale, u, j, h, kv_tile)
                                 for h in range(GQA_GROUP)], axis=1)
            return s, jnp.max(s, axis=0, keepdims=True)

        def trip(jj, inner):
            base = jj * unroll
            s, s_max = scores(base)
            for t in range(unroll):
                if t + 1 < unroll:
                    s_next, s_max_next = scores(base + t + 1)
                m_prev = m_ref[...]
                m_new = jnp.maximum(m_prev, s_max)
                e = jnp.exp2(s - m_new)
                pv = jnp.dot(vt_ref[base + t], e.astype(jnp.bfloat16),
                             preferred_element_type=jnp.float32)
                alpha = jnp.exp2(m_prev - m_new)
                acc_ref[...] = alpha * acc_ref[...] + pv
                l_ref[...] = alpha * l_ref[...] + _sublane_partial_sum(e)
                m_ref[...] = m_new
                if t + 1 < unroll:
                    s, s_max = s_next, s_max_next
            return inner

        lax.fori_loop(0, n_kv // unroll, trip, 0)
        _store_attention([acc_ref[:, c] for c in cols], [l_ref[:, c] for c in cols], o_ref, u)
        return carry

    lax.fori_loop(0, n_sub, sub_tile, 0)


def _attn_call(score_scale, qt, k, vt, *, bounded, batch, seq_len, tq, q_block, kv_tile):
    n_tok = batch * seq_len
    nq = seq_len // q_block
    n_sub = q_block // tq
    kv_blocks = seq_len // kv_tile
    width = GQA_GROUP * tq
    if bounded:
        body = functools.partial(_attn_bounded_kernel, kv_tile=kv_tile,
                                 pass_items=KV_UNROLL)
        scratch = [pltpu.VMEM((GQA_GROUP, kv_tile, tq), jnp.bfloat16),
                   pltpu.VMEM((GQA_GROUP, SUBLANES, tq), jnp.float32),
                   pltpu.VMEM((GQA_GROUP, HEAD_DIM, tq), jnp.float32),
                   pltpu.VMEM((GQA_GROUP, SUBLANES, tq), jnp.float32)]
    else:
        body = functools.partial(_attn_online_kernel, kv_tile=kv_tile)
        scratch = [pltpu.VMEM((HEAD_DIM, width), jnp.float32),
                   pltpu.VMEM((SUBLANES, width), jnp.float32),
                   pltpu.VMEM((1, width), jnp.float32)]
    return pl.pallas_call(
        body,
        grid=(batch, N_KV_HEADS, nq),
        in_specs=[
            pl.BlockSpec(memory_space=pltpu.SMEM),
            pl.BlockSpec((n_sub, GQA_GROUP * SPLIT_DEPTH, tq),
                         lambda b, g, i: (b * nq + i, g, 0)),
            pl.BlockSpec((seq_len, SPLIT_DEPTH), lambda b, g, i: (b, g)),
            pl.BlockSpec((None, kv_blocks, HEAD_DIM, kv_tile), lambda b, g, i: (g, b, 0, 0)),
        ],
        out_specs=pl.BlockSpec((q_block, GQA_GROUP * HEAD_DIM),
                               lambda b, g, i: (b * nq + i, g)),
        out_shape=jax.ShapeDtypeStruct((n_tok, D_MODEL), jnp.bfloat16),
        scratch_shapes=scratch,
        compiler_params=_params(("parallel", "parallel", "arbitrary")),
        name="attn_bounded" if bounded else "attn_online",
    )(score_scale, qt, k, vt)


def _out_kernel(x_ref, attn_ref, sz_ref, sga_ref, cg_ref, wao_ref, wo_ref, g_ref, b_ref,
                y_ref):
    f32 = jnp.float32
    a_in = (attn_ref[...].astype(f32) * sz_ref[...].astype(f32)).astype(jnp.bfloat16)
    a_out = jnp.dot(a_in, wao_ref[...], preferred_element_type=jnp.float32)
    merged = sga_ref[...].astype(f32) * a_out + cg_ref[...]
    out = jnp.dot(merged.astype(jnp.bfloat16), wo_ref[...],
                  preferred_element_type=jnp.float32)
    r = DN_ALPHA * x_ref[...] + out
    mu = jnp.mean(r, axis=-1, keepdims=True)
    d = r - mu
    var = jnp.mean(d * d, axis=-1, keepdims=True)
    y_ref[...] = d * lax.rsqrt(var + LN_EPS) * g_ref[...] + b_ref[...]


def _out_call(x2, attn, sz, sga, cg, w_attn_out, w_o, ln_g, ln_b, *, tm):
    n_tok = x2.shape[0]
    row_spec = pl.BlockSpec((tm, D_MODEL), lambda i: (i, 0))
    return pl.pallas_call(
        _out_kernel,
        grid=(n_tok // tm,),
        in_specs=[row_spec] * 5 + [_full(w_attn_out.shape), _full(w_o.shape),
                                    _full(ln_g.shape), _full(ln_b.shape)],
        out_specs=row_spec,
        out_shape=jax.ShapeDtypeStruct((n_tok, D_MODEL), jnp.float32),
        compiler_params=_params(("parallel",)),
        name="out",
    )(x2, attn, sz, sga, cg, w_attn_out, w_o, ln_g, ln_b)


def _axial_tables_t(seq_len):
    rows = seq_len // GRID_W
    row = jnp.repeat(jnp.arange(rows, dtype=jnp.float32), GRID_W)
    col = jnp.tile(jnp.arange(GRID_W, dtype=jnp.float32), rows)
    inv_freq = ROPE_THETA ** (-jnp.arange(0, ROPE_HALF, 2, dtype=jnp.float32) / ROPE_HALF)
    ang_r = row[:, None] * inv_freq
    ang_c = col[:, None] * inv_freq
    return tuple(t.T for t in (jnp.cos(ang_r), jnp.sin(ang_r), jnp.cos(ang_c), jnp.sin(ang_c)))


def _tile(n, want):
    t = min(n, want)
    assert n % t == 0, (n, t)
    return t


def _layer(x, p):
    batch, seq_len, _ = x.shape
    x2 = x.reshape(batch * seq_len, D_MODEL)
    tm = _tile(seq_len, TOKEN_TILE)
    tq = _tile(seq_len, Q_TILE)
    kv_tile = _tile(seq_len, min(KV_TILE, max(seq_len // KV_UNROLL, Q_TILE)))
    tabs = _axial_tables_t(seq_len)
    qt, k, vt = _qkv_call(x2, p["w_qkv_t"], p["b_qkv"], p["q_gain"], p["k_gain"], tabs,
                          seq_len=seq_len, tm=max(tm, kv_tile), kv_tile=kv_tile, chunk=tq)
    sz, sga, cg = _gates_call(x2, p["w_rest"], p["b_rest"], p["conv_w"], p["conv_b"],
                              p["w_conv_out"], seq_len=seq_len, tm=tm)
    attend = functools.partial(_attn_call, batch=batch, seq_len=seq_len, tq=tq,
                               q_block=_tile(seq_len, Q_BLOCK), kv_tile=kv_tile)
    score_scale = p["score_scale"].reshape(1)
    attn = lax.cond(p["logit_bound"] <= SAFE_LOGIT_BOUND,
                    functools.partial(attend, bounded=True),
                    functools.partial(attend, bounded=False), score_scale, qt, k, vt)
    y = _out_call(x2, attn, sz, sga, cg, p["w_attn_out"], p["w_o"], p["ln_g"], p["ln_b"],
                  tm=tm)
    return y.reshape(batch, seq_len, D_MODEL)


def _prepare(w_in, b_in, q_gain, k_gain, conv_w, conv_b, w_attn_out, w_conv_out, w_o,
             ln_g, ln_b):
    bf = jnp.bfloat16
    tiny = jnp.finfo(jnp.float32).tiny
    q_peak = jnp.maximum(jnp.max(jnp.abs(q_gain)), tiny)
    k_peak = jnp.maximum(jnp.max(jnp.abs(k_gain)), tiny)
    return {
        "w_qkv_t": w_in[:, :QKV_WIDTH].T.astype(bf),
        "b_qkv": b_in[:QKV_WIDTH].reshape(QKV_WIDTH, 1),
        "q_gain": (q_gain * (OPERAND_PEAK / q_peak)).reshape(HEAD_DIM, 1),
        "k_gain": (k_gain * (OPERAND_PEAK / k_peak)).reshape(HEAD_DIM, 1),
        "score_scale": (SCORE_SCALE * LOG2_E / OPERAND_PEAK ** 2) * q_peak * k_peak,
        "logit_bound": math.sqrt(HEAD_DIM) * q_peak * k_peak,
        "w_rest": w_in[:, QKV_WIDTH:].astype(bf),
        "b_rest": b_in[QKV_WIDTH:].reshape(1, -1),
        "conv_w": conv_w,
        "conv_b": conv_b.reshape(1, D_MODEL),
        "w_conv_out": w_conv_out.astype(bf),
        "w_attn_out": w_attn_out.astype(bf),
        "w_o": w_o.astype(bf),
        "ln_g": ln_g.reshape(1, D_MODEL),
        "ln_b": ln_b.reshape(1, D_MODEL),
    }


def kernel(x_prompt, x_sample, w_in, b_in, q_gain, k_gain, conv_w, conv_b, w_attn_out,
           w_conv_out, w_o, ln_g, ln_b):
    depth = w_in.shape[0]
    y_prompt, y_sample = x_prompt, x_sample
    for l in range(depth):
        p = _prepare(w_in[l], b_in[l], q_gain[l], k_gain[l], conv_w[l], conv_b[l],
                     w_attn_out[l], w_conv_out[l], w_o[l], ln_g[l], ln_b[l])
        y_prompt = _layer(y_prompt, p)
        y_sample = _layer(y_sample, p)
    return (y_prompt, y_sample)
```

```python
import functools
import math

import jax
import jax.numpy as jnp
from jax import lax
from jax.experimental import pallas as pl
from jax.experimental.pallas import tpu as pltpu

D_MODEL = 1024
HEAD_DIM = 64
N_HEADS = 16
N_KV_HEADS = 4
GQA_GROUP = N_HEADS // N_KV_HEADS
KV_WIDTH = N_KV_HEADS * HEAD_DIM
QKV_WIDTH = D_MODEL + 2 * KV_WIDTH
GRID_W = 64
ROPE_THETA = 10000.0
ROPE_HALF = HEAD_DIM // 2
ROPE_PAIR = ROPE_HALF // 2
NORM_EPS = 1e-6
LN_EPS = 1e-5
DN_ALPHA = 2.0 ** 0.25
SCORE_SCALE = 1.0 / math.sqrt(HEAD_DIM)
LOG2_E = math.log2(math.e)

SUBLANES = 8
VMEM_LIMIT_BYTES = 56 * 1024 * 1024

FP8 = jnp.float8_e4m3fn
LO_SCALE = 16.0
SPLIT_DEPTH = 4 * HEAD_DIM
OPERAND_PEAK = 16.0
NEG_BIG = -1e30
SAFE_LOGIT_BOUND = 30.0

TOKEN_TILE = 512
QKV_SLAB = 512
HALO = SUBLANES
Q_TILE = 256
Q_BLOCK = 2048
KV_TILE = 512
KV_UNROLL = 16
PASS_ITEMS = 32


def _params(sem):
    return pltpu.CompilerParams(dimension_semantics=sem,
                                vmem_limit_bytes=VMEM_LIMIT_BYTES)


def _full(shape):
    return pl.BlockSpec(shape, lambda *_: (0,) * len(shape))


def _norm_rope_t(u, gain, cr, sr, cc, sc):
    ms = jnp.mean(u * u, axis=0, keepdims=True)
    u = u * lax.rsqrt(ms + NORM_EPS) * gain
    p = ROPE_PAIR
    r1, r2, c1, c2 = u[0:p], u[p:2 * p], u[2 * p:3 * p], u[3 * p:4 * p]
    return jnp.concatenate([r1 * cr - r2 * sr, r2 * cr + r1 * sr,
                            c1 * cc - c2 * sc, c2 * cc + c1 * sc], axis=0)


def _split_fp8(u):
    hi = u.astype(FP8)
    lo = ((u - hi.astype(jnp.float32)) * LO_SCALE).astype(FP8)
    return hi, lo


def _qkv_kernel(x_ref, w_ref, b_ref, qg_ref, kg_ref, cr_ref, sr_ref, cc_ref, sc_ref,
                qt_ref, k_ref, vt_ref, *, kv_tile, chunk):
    tm = x_ref.shape[0]
    qg, kg = qg_ref[...], kg_ref[...]
    d, s = HEAD_DIM, 1.0 / LO_SCALE
    for c in range(tm // chunk):
        tok = slice(c * chunk, (c + 1) * chunk)
        xb = x_ref[tok, :].astype(jnp.bfloat16)
        cr, sr, cc, sc = cr_ref[:, tok], sr_ref[:, tok], cc_ref[:, tok], sc_ref[:, tok]

        def project(lo_row, hi_row, xb=xb):
            return lax.dot_general(w_ref[lo_row:hi_row, :], xb, (((1,), (1,)), ((), ())),
                                   preferred_element_type=jnp.float32) + b_ref[lo_row:hi_row, :]

        for slab in range(D_MODEL // QKV_SLAB):
            pt = project(slab * QKV_SLAB, (slab + 1) * QKV_SLAB)
            for hh in range(QKV_SLAB // d):
                h = slab * (QKV_SLAB // d) + hh
                hi, lo = _split_fp8(_norm_rope_t(pt[hh * d:(hh + 1) * d], qg, cr, sr, cc, sc))
                for part, val in enumerate((hi, hi, lo, lo)):
                    row = h * SPLIT_DEPTH + part * d
                    qt_ref[c, row:row + d, :] = val
        pt = project(D_MODEL, QKV_WIDTH)
        kparts = []
        for g in range(N_KV_HEADS):
            hi, lo = _split_fp8(_norm_rope_t(pt[g * d:(g + 1) * d], kg, cr, sr, cc, sc))
            hi, lo = hi.astype(jnp.float32), lo.astype(jnp.float32)
            kparts += [hi, lo * s, hi * s, lo * (s * s)]
        kt = jnp.concatenate(kparts, axis=0)
        k_ref[tok, :] = kt.T.astype(FP8)
        for g in range(N_KV_HEADS):
            v = pt[KV_WIDTH + g * d:KV_WIDTH + (g + 1) * d]
            tile, off = divmod(c * chunk, kv_tile)
            vt_ref[g, tile, :, off:off + chunk] = v.astype(jnp.bfloat16)


def _qkv_call(x2, w_qkv_t, b_qkv, qg, kg, tabs, *, seq_len, tm, kv_tile, chunk):
    n_tok = x2.shape[0]
    tiles_per_seq = seq_len // tm
    tab_spec = pl.BlockSpec((ROPE_PAIR, tm), lambda i: (0, i % tiles_per_seq))
    return pl.pallas_call(
        functools.partial(_qkv_kernel, kv_tile=kv_tile, chunk=chunk),
        grid=(n_tok // tm,),
        in_specs=[
            pl.BlockSpec((tm, D_MODEL), lambda i: (i, 0)),
            _full((QKV_WIDTH, D_MODEL)),
            _full((QKV_WIDTH, 1)),
            _full((HEAD_DIM, 1)),
            _full((HEAD_DIM, 1)),
            tab_spec, tab_spec, tab_spec, tab_spec,
        ],
        out_specs=[
            pl.BlockSpec((tm // chunk, N_HEADS * SPLIT_DEPTH, chunk), lambda i: (i, 0, 0)),
            pl.BlockSpec((tm, N_KV_HEADS * SPLIT_DEPTH), lambda i: (i, 0)),
            pl.BlockSpec((N_KV_HEADS, tm // kv_tile, HEAD_DIM, kv_tile),
                         lambda i: (0, i, 0, 0)),
        ],
        out_shape=[
            jax.ShapeDtypeStruct((n_tok // chunk, N_HEADS * SPLIT_DEPTH, chunk), FP8),
            jax.ShapeDtypeStruct((n_tok, N_KV_HEADS * SPLIT_DEPTH), FP8),
            jax.ShapeDtypeStruct((N_KV_HEADS, n_tok // kv_tile, HEAD_DIM, kv_tile),
                                 jnp.bfloat16),
        ],
        compiler_params=_params(("parallel",)),
        name="qkv",
    )(x2, w_qkv_t, b_qkv, qg, kg, *tabs)


def _silu(x):
    return x * jax.nn.sigmoid(x)


def _gates_kernel(x_ref, xp_ref, xn_ref, w_ref, b_ref, cw_ref, cb_ref, wco_ref,
                  sz_ref, sga_ref, cg_ref, *, tiles_per_seq):
    tm = x_ref.shape[0]
    i = pl.program_id(0)
    x = x_ref[...]
    xb = x.astype(jnp.bfloat16)
    xcat = jnp.concatenate([xp_ref[...], x, xn_ref[...]], axis=0).astype(jnp.bfloat16)

    def proj(lhs, c):
        sl = slice(c * D_MODEL, (c + 1) * D_MODEL)
        return jnp.dot(lhs, w_ref[:, sl], preferred_element_type=jnp.float32) + b_ref[:, sl]

    sz_ref[...] = _silu(proj(xb, 0)).astype(sz_ref.dtype)
    sga_ref[...] = jax.nn.sigmoid(proj(xb, 5)).astype(sga_ref.dtype)

    u = proj(xcat, 2) * proj(xcat, 3)
    rows = lax.broadcasted_iota(jnp.int32, u.shape, 0)
    first = (i % tiles_per_seq) == 0
    last = (i % tiles_per_seq) == tiles_per_seq - 1
    outside = ((rows < HALO) & first) | ((rows >= tm + HALO) & last)
    u = jnp.where(outside, 0.0, u)
    n = tm + 2 * HALO
    u_prev = pltpu.roll(u, 1, axis=0)[HALO:HALO + tm]
    u_next = pltpu.roll(u, n - 1, axis=0)[HALO:HALO + tm]
    cw = cw_ref[...]
    conv = (u_prev * cw[0:1] + u[HALO:HALO + tm] * cw[1:2] + u_next * cw[2:3]
            + cb_ref[...])
    act = proj(xb, 1) * conv * _silu(proj(xb, 4))
    c_out = jnp.dot(act.astype(jnp.bfloat16), wco_ref[...],
                    preferred_element_type=jnp.float32)
    cg_ref[...] = jax.nn.sigmoid(proj(xb, 6)) * c_out


def _gates_call(x2, w_rest, b_rest, conv_w, conv_b, w_conv_out, *, seq_len, tm):
    n_tok = x2.shape[0]
    halo_blocks = tm // HALO
    last_halo = n_tok // HALO - 1
    row_spec = pl.BlockSpec((tm, D_MODEL), lambda i: (i, 0))
    gate = jax.ShapeDtypeStruct((n_tok, D_MODEL), jnp.bfloat16)
    branch = jax.ShapeDtypeStruct((n_tok, D_MODEL), jnp.float32)
    return pl.pallas_call(
        functools.partial(_gates_kernel, tiles_per_seq=seq_len // tm),
        grid=(n_tok // tm,),
        in_specs=[
            row_spec,
            pl.BlockSpec((HALO, D_MODEL),
                         lambda i: (jnp.maximum(i * halo_blocks - 1, 0), 0)),
            pl.BlockSpec((HALO, D_MODEL),
                         lambda i: (jnp.minimum((i + 1) * halo_blocks, last_halo), 0)),
            _full(w_rest.shape),
            _full(b_rest.shape),
            _full(conv_w.shape),
            _full(conv_b.shape),
            _full(w_conv_out.shape),
        ],
        out_specs=[row_spec, row_spec, row_spec],
        out_shape=[gate, gate, branch],
        compiler_params=_params(("parallel",)),
        name="gates",
    )(x2, x2, x2, w_rest, b_rest, conv_w, conv_b, w_conv_out)


def _scores(k_ref, qt_ref, scale, u, j, h, kv_tile):
    start = pl.multiple_of(j * kv_tile, kv_tile)
    s = jnp.dot(k_ref[pl.ds(start, kv_tile), :],
                qt_ref[u, h * SPLIT_DEPTH:(h + 1) * SPLIT_DEPTH, :],
                preferred_element_type=jnp.float32)
    return s * scale


def _sublane_partial_sum(e):
    return jnp.sum(e.reshape(e.shape[0] // SUBLANES, SUBLANES, e.shape[1]), axis=0)


def _store_attention(acc, row_sum, o_ref, u):
    tq = acc[0].shape[1]
    o = [a / jnp.sum(l, axis=0, keepdims=True) for a, l in zip(acc, row_sum)]
    rows = pl.ds(pl.multiple_of(u * tq, tq), tq)
    o_ref[rows, :] = jnp.concatenate(o, axis=0).T.astype(o_ref.dtype)


def _attn_bounded_kernel(scale_ref, qt_ref, k_ref, vt_ref, o_ref, p_ref, part_ref, acc_ref,
                         l_ref, *, kv_tile, pass_items):
    n_sub, _, tq = qt_ref.shape
    n_kv = k_ref.shape[0] // kv_tile
    scale = scale_ref[0]
    heads = range(GQA_GROUP)
    pass_items = min(pass_items, n_sub * n_kv)
    if n_kv >= pass_items:
        passes = n_kv // pass_items
        n_pass = n_sub * passes

        def item(t, i):
            return t // passes, (t % passes) * pass_items + i

        def starts_sub_tile(t, i):
            return (t >= 0 if passes == 1 else t % passes == 0) if i == 0 else False
    else:
        per_pass = pass_items // n_kv
        n_pass = n_sub // per_pass

        def item(t, i):
            return t * per_pass + i // n_kv, i % n_kv

        def starts_sub_tile(t, i):
            return i % n_kv == 0

    def probs(u, j, h):
        e = jnp.exp2(_scores(k_ref, qt_ref, scale, u, j, h, kv_tile))
        return e.astype(jnp.bfloat16), _sublane_partial_sum(e)

    for h in heads:
        p_ref[h], part_ref[h] = probs(0, 0, h)
        acc_ref[h] = jnp.zeros(acc_ref.shape[1:], jnp.float32)
        l_ref[h] = jnp.ones(l_ref.shape[1:], jnp.float32)

    def one_pass(t, carry):
        t_next = jnp.minimum(t + 1, n_pass - 1)
        acc = [acc_ref[h] for h in heads]
        row_sum = [l_ref[h] for h in heads]
        u_prev, _ = item(jnp.maximum(t - 1, 0), pass_items - 1)
        _store_attention(acc, row_sum, o_ref, u_prev)
        fresh = starts_sub_tile(t, 0)
        acc = [jnp.where(fresh, 0.0, a) for a in acc]
        row_sum = [jnp.where(fresh, 0.0, l) + part_ref[h] for h, l in zip(heads, row_sum)]
        p = [p_ref[h] for h in heads]
        for i in range(pass_items):
            u, j = item(t, i)
            last = i + 1 == pass_items
            u_ahead, j_ahead = item(t_next, 0) if last else item(t, i + 1)
            switch = (not last) and starts_sub_tile(t, i + 1)
            p_next, row_sum_next = [], []
            for h in heads:
                p_h, part = probs(u_ahead, j_ahead, h)
                p_next.append(p_h)
                if last:
                    part_ref[h] = part
                elif switch:
                    row_sum_next.append(part)
                else:
                    row_sum[h] = row_sum[h] + part
                acc[h] = acc[h] + jnp.dot(vt_ref[j], p[h], preferred_element_type=jnp.float32)
            p = p_next
            if switch:
                _store_attention(acc, row_sum, o_ref, u)
                acc = [jnp.zeros_like(a) for a in acc]
                row_sum = row_sum_next
        for h in heads:
            p_ref[h] = p[h]
            acc_ref[h] = acc[h]
            l_ref[h] = row_sum[h]
        return carry

    lax.fori_loop(0, n_pass, one_pass, 0)
    _store_attention([acc_ref[h] for h in heads], [l_ref[h] for h in heads], o_ref, n_sub - 1)


def _attn_online_kernel(scale_ref, qt_ref, k_ref, vt_ref, o_ref, acc_ref, l_ref, m_ref, *,
                        kv_tile):
    n_sub, _, tq = qt_ref.shape
    n_kv = k_ref.shape[0] // kv_tile
    unroll = min(KV_UNROLL, n_kv)
    scale = scale_ref[0]
    cols = [slice(h * tq, (h + 1) * tq) for h in range(GQA_GROUP)]

    def sub_tile(u, carry):
        m_ref[...] = jnp.full(m_ref.shape, NEG_BIG, jnp.float32)
        acc_ref[...] = jnp.zeros(acc_ref.shape, jnp.float32)
        l_ref[...] = jnp.zeros(l_ref.shape, jnp.float32)

        def scores(j):
            s = jnp.concatenate([_scores(k_ref, qt_ref, scale, u, j, h, kv_tile)
                                 for h in range(GQA_GROUP)], axis=1)
            return s, jnp.max(s, axis=0, keepdims=True)

        def trip(jj, inner):
            base = jj * unroll
            s, s_max = scores(base)
            for t in range(unroll):
                if t + 1 < unroll:
                    s_next, s_max_next = scores(base + t + 1)
                m_prev = m_ref[...]
                m_new = jnp.maximum(m_prev, s_max)
                e = jnp.exp2(s - m_new)
                pv = jnp.dot(vt_ref[base + t], e.astype(jnp.bfloat16),
                             preferred_element_type=jnp.float32)
                alpha = jnp.exp2(m_prev - m_new)
                acc_ref[...] = alpha * acc_ref[...] + pv
                l_ref[...] = alpha * l_ref[...] + _sublane_partial_sum(e)
                m_ref[...] = m_new
                if t + 1 < unroll:
                    s, s_max = s_next, s_max_next
            return inner

        lax.fori_loop(0, n_kv // unroll, trip, 0)
        _store_attention([acc_ref[:, c] for c in cols], [l_ref[:, c] for c in cols], o_ref, u)
        return carry

    lax.fori_loop(0, n_sub, sub_tile, 0)


def _attn_call(score_scale, qt, k, vt, *, bounded, batch, seq_len, tq, q_block, kv_tile):
    n_tok = batch * seq_len
    nq = seq_len // q_block
    n_sub = q_block // tq
    kv_blocks = seq_len // kv_tile
    width = GQA_GROUP * tq
    if bounded:
        body = functools.partial(_attn_bounded_kernel, kv_tile=kv_tile,
                                 pass_items=min(PASS_ITEMS, kv_blocks))
        scratch = [pltpu.VMEM((GQA_GROUP, kv_tile, tq), jnp.bfloat16),
                   pltpu.VMEM((GQA_GROUP, SUBLANES, tq), jnp.float32),
                   pltpu.VMEM((GQA_GROUP, HEAD_DIM, tq), jnp.float32),
                   pltpu.VMEM((GQA_GROUP, SUBLANES, tq), jnp.float32)]
    else:
        body = functools.partial(_attn_online_kernel, kv_tile=kv_tile)
        scratch = [pltpu.VMEM((HEAD_DIM, width), jnp.float32),
                   pltpu.VMEM((SUBLANES, width), jnp.float32),
                   pltpu.VMEM((1, width), jnp.float32)]
    return pl.pallas_call(
        body,
        grid=(batch, N_KV_HEADS, nq),
        in_specs=[
            pl.BlockSpec(memory_space=pltpu.SMEM),
            pl.BlockSpec((n_sub, GQA_GROUP * SPLIT_DEPTH, tq),
                         lambda b, g, i: (b * nq + i, g, 0)),
            pl.BlockSpec((seq_len, SPLIT_DEPTH), lambda b, g, i: (b, g)),
            pl.BlockSpec((None, kv_blocks, HEAD_DIM, kv_tile), lambda b, g, i: (g, b, 0, 0)),
        ],
        out_specs=pl.BlockSpec((q_block, GQA_GROUP * HEAD_DIM),
                               lambda b, g, i: (b * nq + i, g)),
        out_shape=jax.ShapeDtypeStruct((n_tok, D_MODEL), jnp.bfloat16),
        scratch_shapes=scratch,
        compiler_params=_params(("parallel", "parallel", "arbitrary")),
        name="attn_bounded" if bounded else "attn_online",
    )(score_scale, qt, k, vt)


def _out_kernel(x_ref, attn_ref, sz_ref, sga_ref, cg_ref, wao_ref, wo_ref, g_ref, b_ref,
                y_ref):
    f32 = jnp.float32
    a_in = (attn_ref[...].astype(f32) * sz_ref[...].astype(f32)).astype(jnp.bfloat16)
    a_out = jnp.dot(a_in, wao_ref[...], preferred_element_type=jnp.float32)
    merged = sga_ref[...].astype(f32) * a_out + cg_ref[...]
    out = jnp.dot(merged.astype(jnp.bfloat16), wo_ref[...],
                  preferred_element_type=jnp.float32)
    r = DN_ALPHA * x_ref[...] + out
    mu = jnp.mean(r, axis=-1, keepdims=True)
    d = r - mu
    var = jnp.mean(d * d, axis=-1, keepdims=True)
    y_ref[...] = d * lax.rsqrt(var + LN_EPS) * g_ref[...] + b_ref[...]


def _out_call(x2, attn, sz, sga, cg, w_attn_out, w_o, ln_g, ln_b, *, tm):
    n_tok = x2.shape[0]
    row_spec = pl.BlockSpec((tm, D_MODEL), lambda i: (i, 0))
    return pl.pallas_call(
        _out_kernel,
        grid=(n_tok // tm,),
        in_specs=[row_spec] * 5 + [_full(w_attn_out.shape), _full(w_o.shape),
                                    _full(ln_g.shape), _full(ln_b.shape)],
        out_specs=row_spec,
        out_shape=jax.ShapeDtypeStruct((n_tok, D_MODEL), jnp.float32),
        compiler_params=_params(("parallel",)),
        name="out",
    )(x2, attn, sz, sga, cg, w_attn_out, w_o, ln_g, ln_b)


def _axial_tables_t(seq_len):
    rows = seq_len // GRID_W
    row = jnp.repeat(jnp.arange(rows, dtype=jnp.float32), GRID_W)
    col = jnp.tile(jnp.arange(GRID_W, dtype=jnp.float32), rows)
    inv_freq = ROPE_THETA ** (-jnp.arange(0, ROPE_HALF, 2, dtype=jnp.float32) / ROPE_HALF)
    ang_r = row[:, None] * inv_freq
    ang_c = col[:, None] * inv_freq
    return tuple(t.T for t in (jnp.cos(ang_r), jnp.sin(ang_r), jnp.cos(ang_c), jnp.sin(ang_c)))


def _tile(n, want):
    t = min(n, want)
    assert n % t == 0, (n, t)
    return t


def _layer(x, p):
    batch, seq_len, _ = x.shape
    x2 = x.reshape(batch * seq_len, D_MODEL)
    tm = _tile(seq_len, TOKEN_TILE)
    tq = _tile(seq_len, Q_TILE)
    kv_tile = _tile(seq_len, min(KV_TILE, max(seq_len // KV_UNROLL, Q_TILE)))
    tabs = _axial_tables_t(seq_len)
    qt, k, vt = _qkv_call(x2, p["w_qkv_t"], p["b_qkv"], p["q_gain"], p["k_gain"], tabs,
                          seq_len=seq_len, tm=max(tm, kv_tile), kv_tile=kv_tile, chunk=tq)
    sz, sga, cg = _gates_call(x2, p["w_rest"], p["b_rest"], p["conv_w"], p["conv_b"],
                              p["w_conv_out"], seq_len=seq_len, tm=tm)
    attend = functools.partial(_attn_call, batch=batch, seq_len=seq_len, tq=tq,
                               q_block=_tile(seq_len, Q_BLOCK), kv_tile=kv_tile)
    score_scale = p["score_scale"].reshape(1)
    attn = lax.cond(p["logit_bound"] <= SAFE_LOGIT_BOUND,
                    functools.partial(attend, bounded=True),
                    functools.partial(attend, bounded=False), score_scale, qt, k, vt)
    y = _out_call(x2, attn, sz, sga, cg, p["w_attn_out"], p["w_o"], p["ln_g"], p["ln_b"],
                  tm=tm)
    return y.reshape(batch, seq_len, D_MODEL)


def _prepare(w_in, b_in, q_gain, k_gain, conv_w, conv_b, w_attn_out, w_conv_out, w_o,
             ln_g, ln_b):
    bf = jnp.bfloat16
    tiny = jnp.finfo(jnp.float32).tiny
    q_peak = jnp.maximum(jnp.max(jnp.abs(q_gain)), tiny)
    k_peak = jnp.maximum(jnp.max(jnp.abs(k_gain)), tiny)
    return {
        "w_qkv_t": w_in[:, :QKV_WIDTH].T.astype(bf),
        "b_qkv": b_in[:QKV_WIDTH].reshape(QKV_WIDTH, 1),
        "q_gain": (q_gain * (OPERAND_PEAK / q_peak)).reshape(HEAD_DIM, 1),
        "k_gain": (k_gain * (OPERAND_PEAK / k_peak)).reshape(HEAD_DIM, 1),
        "score_scale": (SCORE_SCALE * LOG2_E / OPERAND_PEAK ** 2) * q_peak * k_peak,
        "logit_bound": math.sqrt(HEAD_DIM) * q_peak * k_peak,
        "w_rest": w_in[:, QKV_WIDTH:].astype(bf),
        "b_rest": b_in[QKV_WIDTH:].reshape(1, -1),
        "conv_w": conv_w,
        "conv_b": conv_b.reshape(1, D_MODEL),
        "w_conv_out": w_conv_out.astype(bf),
        "w_attn_out": w_attn_out.astype(bf),
        "w_o": w_o.astype(bf),
        "ln_g": ln_g.reshape(1, D_MODEL),
        "ln_b": ln_b.reshape(1, D_MODEL),
    }


def kernel(x_prompt, x_sample, w_in, b_in, q_gain, k_gain, conv_w, conv_b, w_attn_out,
           w_conv_out, w_o, ln_g, ln_b):
    depth = w_in.shape[0]
    y_prompt, y_sample = x_prompt, x_sample
    for l in range(depth):
        p = _prepare(w_in[l], b_in[l], q_gain[l], k_gain[l], conv_w[l], conv_b[l],
                     w_attn_out[l], w_conv_out[l], w_o[l], ln_g[l], ln_b[l])
        y_prompt = _layer(y_prompt, p)
        y_sample = _layer(y_sample, p)
    return (y_prompt, y_sample)
```

```python
import functools
import math

import jax
import jax.numpy as jnp
from jax import lax
from jax.experimental import pallas as pl
from jax.experimental.pallas import tpu as pltpu

D_MODEL = 1024
HEAD_DIM = 64
N_HEADS = 16
N_KV_HEADS = 4
GQA_GROUP = N_HEADS // N_KV_HEADS
KV_WIDTH = N_KV_HEADS * HEAD_DIM
QKV_WIDTH = D_MODEL + 2 * KV_WIDTH
GRID_W = 64
ROPE_THETA = 10000.0
ROPE_HALF = HEAD_DIM // 2
ROPE_PAIR = ROPE_HALF // 2
NORM_EPS = 1e-6
LN_EPS = 1e-5
DN_ALPHA = 2.0 ** 0.25
SCORE_SCALE = 1.0 / math.sqrt(HEAD_DIM)
LOG2_E = math.log2(math.e)

SUBLANES = 8
VMEM_LIMIT_BYTES = 56 * 1024 * 1024

FP8 = jnp.float8_e4m3fn
LO_SCALE = 16.0
SPLIT_DEPTH = 4 * HEAD_DIM
OPERAND_PEAK = 16.0
NEG_BIG = -1e30
SAFE_LOGIT_BOUND = 30.0

TOKEN_TILE = 512
QKV_SLAB = 512
HALO = SUBLANES
Q_TILE = 256
Q_BLOCK = 2048
KV_TILE = 512
KV_UNROLL = 16
PASS_ITEMS = 32


def _params(sem):
    return pltpu.CompilerParams(dimension_semantics=sem,
                                vmem_limit_bytes=VMEM_LIMIT_BYTES)


def _full(shape):
    return pl.BlockSpec(shape, lambda *_: (0,) * len(shape))


def _norm_rope_t(u, gain, cr, sr, cc, sc):
    ms = jnp.mean(u * u, axis=0, keepdims=True)
    u = u * lax.rsqrt(ms + NORM_EPS) * gain
    p = ROPE_PAIR
    r1, r2, c1, c2 = u[0:p], u[p:2 * p], u[2 * p:3 * p], u[3 * p:4 * p]
    return jnp.concatenate([r1 * cr - r2 * sr, r2 * cr + r1 * sr,
                            c1 * cc - c2 * sc, c2 * cc + c1 * sc], axis=0)


def _split_fp8(u):
    hi = u.astype(FP8)
    lo = ((u - hi.astype(jnp.float32)) * LO_SCALE).astype(FP8)
    return hi, lo


def _qkv_kernel(x_ref, w_ref, b_ref, qg_ref, kg_ref, cr_ref, sr_ref, cc_ref, sc_ref,
                qt_ref, k_ref, vt_ref, *, kv_tile, chunk):
    tm = x_ref.shape[0]
    qg, kg = qg_ref[...], kg_ref[...]
    d, s = HEAD_DIM, 1.0 / LO_SCALE
    for c in range(tm // chunk):
        tok = slice(c * chunk, (c + 1) * chunk)
        xb = x_ref[tok, :].astype(jnp.bfloat16)
        cr, sr, cc, sc = cr_ref[:, tok], sr_ref[:, tok], cc_ref[:, tok], sc_ref[:, tok]

        def project(lo_row, hi_row, xb=xb):
            return lax.dot_general(w_ref[lo_row:hi_row, :], xb, (((1,), (1,)), ((), ())),
                                   preferred_element_type=jnp.float32) + b_ref[lo_row:hi_row, :]

        for slab in range(D_MODEL // QKV_SLAB):
            pt = project(slab * QKV_SLAB, (slab + 1) * QKV_SLAB)
            for hh in range(QKV_SLAB // d):
                h = slab * (QKV_SLAB // d) + hh
                hi, lo = _split_fp8(_norm_rope_t(pt[hh * d:(hh + 1) * d], qg, cr, sr, cc, sc))
                for part, val in enumerate((hi, hi, lo, lo)):
                    row = h * SPLIT_DEPTH + part * d
                    qt_ref[c, row:row + d, :] = val
        pt = project(D_MODEL, QKV_WIDTH)
        kparts = []
        for g in range(N_KV_HEADS):
            hi, lo = _split_fp8(_norm_rope_t(pt[g * d:(g + 1) * d], kg, cr, sr, cc, sc))
            hi, lo = hi.astype(jnp.float32), lo.astype(jnp.float32)
            kparts += [hi, lo * s, hi * s, lo * (s * s)]
        kt = jnp.concatenate(kparts, axis=0)
        k_ref[tok, :] = kt.T.astype(FP8)
        for g in range(N_KV_HEADS):
            v = pt[KV_WIDTH + g * d:KV_WIDTH + (g + 1) * d]
            tile, off = divmod(c * chunk, kv_tile)
            vt_ref[g, tile, :, off:off + chunk] = v.astype(jnp.bfloat16)


def _qkv_call(x2, w_qkv_t, b_qkv, qg, kg, tabs, *, seq_len, tm, kv_tile, chunk):
    n_tok = x2.shape[0]
    tiles_per_seq = seq_len // tm
    tab_spec = pl.BlockSpec((ROPE_PAIR, tm), lambda i: (0, i % tiles_per_seq))
    return pl.pallas_call(
        functools.partial(_qkv_kernel, kv_tile=kv_tile, chunk=chunk),
        grid=(n_tok // tm,),
        in_specs=[
            pl.BlockSpec((tm, D_MODEL), lambda i: (i, 0)),
            _full((QKV_WIDTH, D_MODEL)),
            _full((QKV_WIDTH, 1)),
            _full((HEAD_DIM, 1)),
            _full((HEAD_DIM, 1)),
            tab_spec, tab_spec, tab_spec, tab_spec,
        ],
        out_specs=[
            pl.BlockSpec((tm // chunk, N_HEADS * SPLIT_DEPTH, chunk), lambda i: (i, 0, 0)),
            pl.BlockSpec((tm, N_KV_HEADS * SPLIT_DEPTH), lambda i: (i, 0)),
            pl.BlockSpec((N_KV_HEADS, tm // kv_tile, HEAD_DIM, kv_tile),
                         lambda i: (0, i, 0, 0)),
        ],
        out_shape=[
            jax.ShapeDtypeStruct((n_tok // chunk, N_HEADS * SPLIT_DEPTH, chunk), FP8),
            jax.ShapeDtypeStruct((n_tok, N_KV_HEADS * SPLIT_DEPTH), FP8),
            jax.ShapeDtypeStruct((N_KV_HEADS, n_tok // kv_tile, HEAD_DIM, kv_tile),
                                 jnp.bfloat16),
        ],
        compiler_params=_params(("parallel",)),
        name="qkv",
    )(x2, w_qkv_t, b_qkv, qg, kg, *tabs)


def _silu(x):
    return x * jax.nn.sigmoid(x)


def _gates_kernel(x_ref, xp_ref, xn_ref, w_ref, b_ref, cw_ref, cb_ref, wco_ref,
                  sz_ref, sga_ref, cg_ref, *, tiles_per_seq):
    tm = x_ref.shape[0]
    i = pl.program_id(0)
    x = x_ref[...]
    xb = x.astype(jnp.bfloat16)
    xcat = jnp.concatenate([xp_ref[...], x, xn_ref[...]], axis=0).astype(jnp.bfloat16)

    def proj(lhs, c):
        sl = slice(c * D_MODEL, (c + 1) * D_MODEL)
        return jnp.dot(lhs, w_ref[:, sl], preferred_element_type=jnp.float32) + b_ref[:, sl]

    sz_ref[...] = _silu(proj(xb, 0)).astype(sz_ref.dtype)
    sga_ref[...] = jax.nn.sigmoid(proj(xb, 5)).astype(sga_ref.dtype)

    u = proj(xcat, 2) * proj(xcat, 3)
    rows = lax.broadcasted_iota(jnp.int32, u.shape, 0)
    first = (i % tiles_per_seq) == 0
    last = (i % tiles_per_seq) == tiles_per_seq - 1
    outside = ((rows < HALO) & first) | ((rows >= tm + HALO) & last)
    u = jnp.where(outside, 0.0, u)
    n = tm + 2 * HALO
    u_prev = pltpu.roll(u, 1, axis=0)[HALO:HALO + tm]
    u_next = pltpu.roll(u, n - 1, axis=0)[HALO:HALO + tm]
    cw = cw_ref[...]
    conv = (u_prev * cw[0:1] + u[HALO:HALO + tm] * cw[1:2] + u_next * cw[2:3]
            + cb_ref[...])
    act = proj(xb, 1) * conv * _silu(proj(xb, 4))
    c_out = jnp.dot(act.astype(jnp.bfloat16), wco_ref[...],
                    preferred_element_type=jnp.float32)
    cg_ref[...] = jax.nn.sigmoid(proj(xb, 6)) * c_out


def _gates_call(x2, w_rest, b_rest, conv_w, conv_b, w_conv_out, *, seq_len, tm):
    n_tok = x2.shape[0]
    halo_blocks = tm // HALO
    last_halo = n_tok // HALO - 1
    row_spec = pl.BlockSpec((tm, D_MODEL), lambda i: (i, 0))
    gate = jax.ShapeDtypeStruct((n_tok, D_MODEL), jnp.bfloat16)
    branch = jax.ShapeDtypeStruct((n_tok, D_MODEL), jnp.float32)
    return pl.pallas_call(
        functools.partial(_gates_kernel, tiles_per_seq=seq_len // tm),
        grid=(n_tok // tm,),
        in_specs=[
            row_spec,
            pl.BlockSpec((HALO, D_MODEL),
                         lambda i: (jnp.maximum(i * halo_blocks - 1, 0), 0)),
            pl.BlockSpec((HALO, D_MODEL),
                         lambda i: (jnp.minimum((i + 1) * halo_blocks, last_halo), 0)),
            _full(w_rest.shape),
            _full(b_rest.shape),
            _full(conv_w.shape),
            _full(conv_b.shape),
            _full(w_conv_out.shape),
        ],
        out_specs=[row_spec, row_spec, row_spec],
        out_shape=[gate, gate, branch],
        compiler_params=_params(("parallel",)),
        name="gates",
    )(x2, x2, x2, w_rest, b_rest, conv_w, conv_b, w_conv_out)


def _scores(k_ref, qt_ref, scale, u, j, h, kv_tile):
    start = pl.multiple_of(j * kv_tile, kv_tile)
    s = jnp.dot(k_ref[pl.ds(start, kv_tile), :],
                qt_ref[u, h * SPLIT_DEPTH:(h + 1) * SPLIT_DEPTH, :],
                preferred_element_type=jnp.float32)
    return s * scale


def _sublane_partial_sum(e):
    return jnp.sum(e.reshape(e.shape[0] // SUBLANES, SUBLANES, e.shape[1]), axis=0)


def _store_attention(acc, row_sum, o_ref, u):
    tq = acc[0].shape[1]
    o = [a / jnp.sum(l, axis=0, keepdims=True) for a, l in zip(acc, row_sum)]
    rows = pl.ds(pl.multiple_of(u * tq, tq), tq)
    o_ref[rows, :] = jnp.concatenate(o, axis=0).T.astype(o_ref.dtype)


def _attn_bounded_kernel(scale_ref, qt_ref, k_ref, vt_ref, o_ref, p_ref, part_ref, acc_ref,
                         l_ref, *, kv_tile, pass_items):
    n_sub, _, tq = qt_ref.shape
    n_kv = k_ref.shape[0] // kv_tile
    scale = scale_ref[0]
    heads = range(GQA_GROUP)
    pass_items = min(pass_items, n_sub * n_kv)
    if n_kv >= pass_items:
        passes = n_kv // pass_items
        n_pass = n_sub * passes

        def item(t, i):
            return t // passes, (t % passes) * pass_items + i

        def starts_sub_tile(t, i):
            return (t >= 0 if passes == 1 else t % passes == 0) if i == 0 else False
    else:
        per_pass = pass_items // n_kv
        n_pass = n_sub // per_pass

        def item(t, i):
            return t * per_pass + i // n_kv, i % n_kv

        def starts_sub_tile(t, i):
            return i % n_kv == 0

    def probs(u, j, h):
        e = jnp.exp2(_scores(k_ref, qt_ref, scale, u, j, h, kv_tile))
        return e.astype(jnp.bfloat16), _sublane_partial_sum(e)

    for h in heads:
        p_ref[h], part_ref[h] = probs(0, 0, h)
        acc_ref[h] = jnp.zeros(acc_ref.shape[1:], jnp.float32)
        l_ref[h] = jnp.ones(l_ref.shape[1:], jnp.float32)

    def one_pass(t, carry):
        t_next = jnp.minimum(t + 1, n_pass - 1)
        acc = [acc_ref[h] for h in heads]
        row_sum = [l_ref[h] for h in heads]
        u_prev, _ = item(jnp.maximum(t - 1, 0), pass_items - 1)
        _store_attention(acc, row_sum, o_ref, u_prev)
        fresh = starts_sub_tile(t, 0)
        acc = [jnp.where(fresh, 0.0, a) for a in acc]
        row_sum = [jnp.where(fresh, 0.0, l) + part_ref[h] for h, l in zip(heads, row_sum)]
        p = [p_ref[h] for h in heads]
        for i in range(pass_items):
            u, j = item(t, i)
            last = i + 1 == pass_items
            u_ahead, j_ahead = item(t_next, 0) if last else item(t, i + 1)
            switch = (not last) and starts_sub_tile(t, i + 1)
            p_next, row_sum_next = [], []
            for h in heads:
                p_h, part = probs(u_ahead, j_ahead, h)
                p_next.append(p_h)
                if last:
                    part_ref[h] = part
                elif switch:
                    row_sum_next.append(part)
                else:
                    row_sum[h] = row_sum[h] + part
                acc[h] = acc[h] + jnp.dot(vt_ref[j], p[h], preferred_element_type=jnp.float32)
            p = p_next
            if switch:
                _store_attention(acc, row_sum, o_ref, u)
                acc = [jnp.zeros_like(a) for a in acc]
                row_sum = row_sum_next
        for h in heads:
            p_ref[h] = p[h]
            acc_ref[h] = acc[h]
            l_ref[h] = row_sum[h]
        return carry

    lax.fori_loop(0, n_pass, one_pass, 0)
    _store_attention([acc_ref[h] for h in heads], [l_ref[h] for h in heads], o_ref, n_sub - 1)


def _attn_online_kernel(scale_ref, qt_ref, k_ref, vt_ref, o_ref, acc_ref, l_ref, m_ref, *,
                        kv_tile):
    n_sub, _, tq = qt_ref.shape
    n_kv = k_ref.shape[0] // kv_tile
    unroll = min(KV_UNROLL, n_kv)
    scale = scale_ref[0]
    cols = [slice(h * tq, (h + 1) * tq) for h in range(GQA_GROUP)]

    def sub_tile(u, carry):
        m_ref[...] = jnp.full(m_ref.shape, NEG_BIG, jnp.float32)
        acc_ref[...] = jnp.zeros(acc_ref.shape, jnp.float32)
        l_ref[...] = jnp.zeros(l_ref.shape, jnp.float32)

        def scores(j):
            s = jnp.concatenate([_scores(k_ref, qt_ref, scale, u, j, h, kv_tile)
                                 for h in range(GQA_GROUP)], axis=1)
            return s, jnp.max(s, axis=0, keepdims=True)

        def trip(jj, inner):
            base = jj * unroll
            s, s_max = scores(base)
            for t in range(unroll):
                if t + 1 < unroll:
                    s_next, s_max_next = scores(base + t + 1)
                m_prev = m_ref[...]
                m_new = jnp.maximum(m_prev, s_max)
                e = jnp.exp2(s - m_new)
                pv = jnp.dot(vt_ref[base + t], e.astype(jnp.bfloat16),
                             preferred_element_type=jnp.float32)
                alpha = jnp.exp2(m_prev - m_new)
                acc_ref[...] = alpha * acc_ref[...] + pv
                l_ref[...] = alpha * l_ref[...] + _sublane_partial_sum(e)
                m_ref[...] = m_new
                if t + 1 < unroll:
                    s, s_max = s_next, s_max_next
            return inner

        lax.fori_loop(0, n_kv // unroll, trip, 0)
        _store_attention([acc_ref[:, c] for c in cols], [l_ref[:, c] for c in cols], o_ref, u)
        return carry

    lax.fori_loop(0, n_sub, sub_tile, 0)


def _attn_call(score_scale, qt, k, vt, *, bounded, batch, seq_len, tq, q_block, kv_tile):
    n_tok = batch * seq_len
    nq = seq_len // q_block
    n_sub = q_block // tq
    kv_blocks = seq_len // kv_tile
    width = GQA_GROUP * tq
    if bounded:
        body = functools.partial(_attn_bounded_kernel, kv_tile=kv_tile,
                                 pass_items=min(PASS_ITEMS, kv_blocks))
        scratch = [pltpu.VMEM((GQA_GROUP, kv_tile, tq), jnp.bfloat16),
                   pltpu.VMEM((GQA_GROUP, SUBLANES, tq), jnp.float32),
                   pltpu.VMEM((GQA_GROUP, HEAD_DIM, tq), jnp.float32),
                   pltpu.VMEM((GQA_GROUP, SUBLANES, tq), jnp.float32)]
    else:
        body = functools.partial(_attn_online_kernel, kv_tile=kv_tile)
        scratch = [pltpu.VMEM((HEAD_DIM, width), jnp.float32),
                   pltpu.VMEM((SUBLANES, width), jnp.float32),
                   pltpu.VMEM((1, width), jnp.float32)]
    return pl.pallas_call(
        body,
        grid=(batch, N_KV_HEADS, nq),
        in_specs=[
            pl.BlockSpec(memory_space=pltpu.SMEM),
            pl.BlockSpec((n_sub, GQA_GROUP * SPLIT_DEPTH, tq),
                         lambda b, g, i: (b * nq + i, g, 0)),
            pl.BlockSpec((seq_len, SPLIT_DEPTH), lambda b, g, i: (b, g)),
            pl.BlockSpec((None, kv_blocks, HEAD_DIM, kv_tile), lambda b, g, i: (g, b, 0, 0)),
        ],
        out_specs=pl.BlockSpec((q_block, GQA_GROUP * HEAD_DIM),
                               lambda b, g, i: (b * nq + i, g)),
        out_shape=jax.ShapeDtypeStruct((n_tok, D_MODEL), jnp.bfloat16),
        scratch_shapes=scratch,
        compiler_params=_params(("parallel", "parallel", "arbitrary")),
        name="attn_bounded" if bounded else "attn_online",
    )(score_scale, qt, k, vt)


def _out_kernel(x_ref, attn_ref, sz_ref, sga_ref, cg_ref, wao_ref, wo_ref, g_ref, b_ref,
                y_ref):
    f32 = jnp.float32
    a_in = (attn_ref[...].astype(f32) * sz_ref[...].astype(f32)).astype(jnp.bfloat16)
    a_out = jnp.dot(a_in, wao_ref[...], preferred_element_type=jnp.float32)
    merged = sga_ref[...].astype(f32) * a_out + cg_ref[...]
    out = jnp.dot(merged.astype(jnp.bfloat16), wo_ref[...],
                  preferred_element_type=jnp.float32)
    r = DN_ALPHA * x_ref[...] + out
    mu = jnp.mean(r, axis=-1, keepdims=True)
    d = r - mu
    var = jnp.mean(d * d, axis=-1, keepdims=True)
    y_ref[...] = d * lax.rsqrt(var + LN_EPS) * g_ref[...] + b_ref[...]


def _out_call(x2, attn, sz, sga, cg, w_attn_out, w_o, ln_g, ln_b, *, tm):
    n_tok = x2.shape[0]
    row_spec = pl.BlockSpec((tm, D_MODEL), lambda i: (i, 0))
    return pl.pallas_call(
        _out_kernel,
        grid=(n_tok // tm,),
        in_specs=[row_spec] * 5 + [_full(w_attn_out.shape), _full(w_o.shape),
                                    _full(ln_g.shape), _full(ln_b.shape)],
        out_specs=row_spec,
        out_shape=jax.ShapeDtypeStruct((n_tok, D_MODEL), jnp.float32),
        compiler_params=_params(("parallel",)),
        name="out",
    )(x2, attn, sz, sga, cg, w_attn_out, w_o, ln_g, ln_b)


def _axial_tables_t(seq_len):
    rows = seq_len // GRID_W
    inv_freq = ROPE_THETA ** (-jnp.arange(0, ROPE_HALF, 2, dtype=jnp.float32) / ROPE_HALF)
    ang_r = inv_freq[:, None] * jnp.arange(rows, dtype=jnp.float32)
    ang_c = inv_freq[:, None] * jnp.arange(GRID_W, dtype=jnp.float32)
    by_row = lambda t: jnp.repeat(t, GRID_W, axis=1)
    by_col = lambda t: jnp.tile(t, (1, rows))
    return (by_row(jnp.cos(ang_r)), by_row(jnp.sin(ang_r)),
            by_col(jnp.cos(ang_c)), by_col(jnp.sin(ang_c)))


def _tile(n, want):
    t = min(n, want)
    assert n % t == 0, (n, t)
    return t


def _layer(x, p):
    batch, seq_len, _ = x.shape
    x2 = x.reshape(batch * seq_len, D_MODEL)
    tm = _tile(seq_len, TOKEN_TILE)
    tq = _tile(seq_len, Q_TILE)
    kv_tile = _tile(seq_len, min(KV_TILE, max(seq_len // KV_UNROLL, Q_TILE)))
    tabs = _axial_tables_t(seq_len)
    qt, k, vt = _qkv_call(x2, p["w_qkv_t"], p["b_qkv"], p["q_gain"], p["k_gain"], tabs,
                          seq_len=seq_len, tm=max(tm, kv_tile), kv_tile=kv_tile, chunk=tq)
    sz, sga, cg = _gates_call(x2, p["w_rest"], p["b_rest"], p["conv_w"], p["conv_b"],
                              p["w_conv_out"], seq_len=seq_len, tm=tm)
    attend = functools.partial(_attn_call, batch=batch, seq_len=seq_len, tq=tq,
                               q_block=_tile(seq_len, Q_BLOCK), kv_tile=kv_tile)
    score_scale = p["score_scale"].reshape(1)
    attn = lax.cond(p["logit_bound"] <= SAFE_LOGIT_BOUND,
                    functools.partial(attend, bounded=True),
                    functools.partial(attend, bounded=False), score_scale, qt, k, vt)
    y = _out_call(x2, attn, sz, sga, cg, p["w_attn_out"], p["w_o"], p["ln_g"], p["ln_b"],
                  tm=tm)
    return y.reshape(batch, seq_len, D_MODEL)


def _prepare(w_in, b_in, q_gain, k_gain, conv_w, conv_b, w_attn_out, w_conv_out, w_o,
             ln_g, ln_b):
    bf = jnp.bfloat16
    tiny = jnp.finfo(jnp.float32).tiny
    q_peak = jnp.maximum(jnp.max(jnp.abs(q_gain)), tiny)
    k_peak = jnp.maximum(jnp.max(jnp.abs(k_gain)), tiny)
    return {
        "w_qkv_t": w_in[:, :QKV_WIDTH].T.astype(bf),
        "b_qkv": b_in[:QKV_WIDTH].reshape(QKV_WIDTH, 1),
        "q_gain": (q_gain * (OPERAND_PEAK / q_peak)).reshape(HEAD_DIM, 1),
        "k_gain": (k_gain * (OPERAND_PEAK / k_peak)).reshape(HEAD_DIM, 1),
        "score_scale": (SCORE_SCALE * LOG2_E / OPERAND_PEAK ** 2) * q_peak * k_peak,
        "logit_bound": math.sqrt(HEAD_DIM) * q_peak * k_peak,
        "w_rest": w_in[:, QKV_WIDTH:].astype(bf),
        "b_rest": b_in[QKV_WIDTH:].reshape(1, -1),
        "conv_w": conv_w,
        "conv_b": conv_b.reshape(1, D_MODEL),
        "w_conv_out": w_conv_out.astype(bf),
        "w_attn_out": w_attn_out.astype(bf),
        "w_o": w_o.astype(bf),
        "ln_g": ln_g.reshape(1, D_MODEL),
        "ln_b": ln_b.reshape(1, D_MODEL),
    }


def kernel(x_prompt, x_sample, w_in, b_in, q_gain, k_gain, conv_w, conv_b, w_attn_out,
           w_conv_out, w_o, ln_g, ln_b):
    depth = w_in.shape[0]
    y_prompt, y_sample = x_prompt, x_sample
    for l in range(depth):
        p = _prepare(w_in[l], b_in[l], q_gain[l], k_gain[l], conv_w[l], conv_b[l],
                     w_attn_out[l], w_conv_out[l], w_o[l], ln_g[l], ln_b[l])
        y_prompt = _layer(y_prompt, p)
        y_sample = _layer(y_sample, p)
    return (y_prompt, y_sample)
```

```python
import functools
import math

import jax
import jax.numpy as jnp
from jax import lax
from jax.experimental import pallas as pl
from jax.experimental.pallas import tpu as pltpu

D_MODEL = 1024
HEAD_DIM = 64
N_HEADS = 16
N_KV_HEADS = 4
GQA_GROUP = N_HEADS // N_KV_HEADS
KV_WIDTH = N_KV_HEADS * HEAD_DIM
QKV_WIDTH = D_MODEL + 2 * KV_WIDTH
GRID_W = 64
ROPE_THETA = 10000.0
ROPE_HALF = HEAD_DIM // 2
ROPE_PAIR = ROPE_HALF // 2
NORM_EPS = 1e-6
LN_EPS = 1e-5
DN_ALPHA = 2.0 ** 0.25
SCORE_SCALE = 1.0 / math.sqrt(HEAD_DIM)
LOG2_E = math.log2(math.e)

SUBLANES = 8
VMEM_LIMIT_BYTES = 56 * 1024 * 1024

FP8 = jnp.float8_e4m3fn
LO_SCALE = 16.0
SPLIT_DEPTH = 4 * HEAD_DIM
OPERAND_PEAK = 16.0
NEG_BIG = -1e30
SAFE_LOGIT_BOUND = 30.0

TOKEN_TILE = 512
QKV_SLAB = 512
HALO = SUBLANES
Q_TILE = 256
Q_BLOCK = 2048
KV_TILE = 512
KV_UNROLL = 16
PASS_ITEMS = 32


def _params(sem):
    return pltpu.CompilerParams(dimension_semantics=sem,
                                vmem_limit_bytes=VMEM_LIMIT_BYTES)


def _full(shape):
    return pl.BlockSpec(shape, lambda *_: (0,) * len(shape))


def _norm_rope_t(u, gain, cr, sr, cc, sc):
    ms = jnp.mean(u * u, axis=0, keepdims=True)
    u = u * lax.rsqrt(ms + NORM_EPS) * gain
    p = ROPE_PAIR
    r1, r2, c1, c2 = u[0:p], u[p:2 * p], u[2 * p:3 * p], u[3 * p:4 * p]
    return jnp.concatenate([r1 * cr - r2 * sr, r2 * cr + r1 * sr,
                            c1 * cc - c2 * sc, c2 * cc + c1 * sc], axis=0)


def _split_fp8(u):
    hi = u.astype(FP8)
    lo = ((u - hi.astype(jnp.float32)) * LO_SCALE).astype(FP8)
    return hi, lo


def _qkv_kernel(x_ref, w_ref, b_ref, qg_ref, kg_ref, cr_ref, sr_ref, cc_ref, sc_ref,
                qt_ref, k_ref, vt_ref, *, kv_tile, chunk):
    tm = x_ref.shape[0]
    qg, kg = qg_ref[...], kg_ref[...]
    d, s = HEAD_DIM, 1.0 / LO_SCALE
    for c in range(tm // chunk):
        tok = slice(c * chunk, (c + 1) * chunk)
        xb = x_ref[tok, :].astype(jnp.bfloat16)
        cr, sr, cc, sc = cr_ref[:, tok], sr_ref[:, tok], cc_ref[:, tok], sc_ref[:, tok]

        def project(lo_row, hi_row, xb=xb):
            return lax.dot_general(w_ref[lo_row:hi_row, :], xb, (((1,), (1,)), ((), ())),
                                   preferred_element_type=jnp.float32) + b_ref[lo_row:hi_row, :]

        for slab in range(D_MODEL // QKV_SLAB):
            pt = project(slab * QKV_SLAB, (slab + 1) * QKV_SLAB)
            for hh in range(QKV_SLAB // d):
                h = slab * (QKV_SLAB // d) + hh
                hi, lo = _split_fp8(_norm_rope_t(pt[hh * d:(hh + 1) * d], qg, cr, sr, cc, sc))
                for part, val in enumerate((hi, hi, lo, lo)):
                    row = h * SPLIT_DEPTH + part * d
                    qt_ref[c, row:row + d, :] = val
        pt = project(D_MODEL, QKV_WIDTH)
        kparts = []
        for g in range(N_KV_HEADS):
            hi, lo = _split_fp8(_norm_rope_t(pt[g * d:(g + 1) * d], kg, cr, sr, cc, sc))
            hi, lo = hi.astype(jnp.float32), lo.astype(jnp.float32)
            kparts += [hi, lo * s, hi * s, lo * (s * s)]
        kt = jnp.concatenate(kparts, axis=0)
        k_ref[tok, :] = kt.T.astype(FP8)
        for g in range(N_KV_HEADS):
            v = pt[KV_WIDTH + g * d:KV_WIDTH + (g + 1) * d]
            tile, off = divmod(c * chunk, kv_tile)
            vt_ref[g, tile, :, off:off + chunk] = v.astype(jnp.bfloat16)


def _qkv_call(x2, w_qkv_t, b_qkv, qg, kg, tabs, *, seq_len, tm, kv_tile, chunk):
    n_tok = x2.shape[0]
    tiles_per_seq = seq_len // tm
    tab_spec = pl.BlockSpec((ROPE_PAIR, tm), lambda i: (0, i % tiles_per_seq))
    return pl.pallas_call(
        functools.partial(_qkv_kernel, kv_tile=kv_tile, chunk=chunk),
        grid=(n_tok // tm,),
        in_specs=[
            pl.BlockSpec((tm, D_MODEL), lambda i: (i, 0)),
            _full((QKV_WIDTH, D_MODEL)),
            _full((QKV_WIDTH, 1)),
            _full((HEAD_DIM, 1)),
            _full((HEAD_DIM, 1)),
            tab_spec, tab_spec, tab_spec, tab_spec,
        ],
        out_specs=[
            pl.BlockSpec((tm // chunk, N_HEADS * SPLIT_DEPTH, chunk), lambda i: (i, 0, 0)),
            pl.BlockSpec((tm, N_KV_HEADS * SPLIT_DEPTH), lambda i: (i, 0)),
            pl.BlockSpec((N_KV_HEADS, tm // kv_tile, HEAD_DIM, kv_tile),
                         lambda i: (0, i, 0, 0)),
        ],
        out_shape=[
            jax.ShapeDtypeStruct((n_tok // chunk, N_HEADS * SPLIT_DEPTH, chunk), FP8),
            jax.ShapeDtypeStruct((n_tok, N_KV_HEADS * SPLIT_DEPTH), FP8),
            jax.ShapeDtypeStruct((N_KV_HEADS, n_tok // kv_tile, HEAD_DIM, kv_tile),
                                 jnp.bfloat16),
        ],
        compiler_params=_params(("parallel",)),
        name="qkv",
    )(x2, w_qkv_t, b_qkv, qg, kg, *tabs)


def _silu(x):
    return x * jax.nn.sigmoid(x)


def _gates_kernel(x_ref, xp_ref, xn_ref, w_ref, b_ref, cw_ref, cb_ref, wco_ref,
                  sz_ref, sga_ref, cg_ref, *, tiles_per_seq):
    tm = x_ref.shape[0]
    i = pl.program_id(0)
    x = x_ref[...]
    xb = x.astype(jnp.bfloat16)
    xcat = jnp.concatenate([xp_ref[...], x, xn_ref[...]], axis=0).astype(jnp.bfloat16)

    def proj(lhs, c):
        sl = slice(c * D_MODEL, (c + 1) * D_MODEL)
        return jnp.dot(lhs, w_ref[:, sl], preferred_element_type=jnp.float32) + b_ref[:, sl]

    sz_ref[...] = _silu(proj(xb, 0)).astype(sz_ref.dtype)
    sga_ref[...] = jax.nn.sigmoid(proj(xb, 5)).astype(sga_ref.dtype)

    u = proj(xcat, 2) * proj(xcat, 3)
    rows = lax.broadcasted_iota(jnp.int32, u.shape, 0)
    first = (i % tiles_per_seq) == 0
    last = (i % tiles_per_seq) == tiles_per_seq - 1
    outside = ((rows < HALO) & first) | ((rows >= tm + HALO) & last)
    u = jnp.where(outside, 0.0, u)
    n = tm + 2 * HALO
    u_prev = pltpu.roll(u, 1, axis=0)[HALO:HALO + tm]
    u_next = pltpu.roll(u, n - 1, axis=0)[HALO:HALO + tm]
    cw = cw_ref[...]
    conv = (u_prev * cw[0:1] + u[HALO:HALO + tm] * cw[1:2] + u_next * cw[2:3]
            + cb_ref[...])
    act = proj(xb, 1) * conv * _silu(proj(xb, 4))
    c_out = jnp.dot(act.astype(jnp.bfloat16), wco_ref[...],
                    preferred_element_type=jnp.float32)
    cg_ref[...] = (jax.nn.sigmoid(proj(xb, 6)) * c_out).astype(cg_ref.dtype)


def _gates_call(x2, w_rest, b_rest, conv_w, conv_b, w_conv_out, *, seq_len, tm):
    n_tok = x2.shape[0]
    halo_blocks = tm // HALO
    last_halo = n_tok // HALO - 1
    row_spec = pl.BlockSpec((tm, D_MODEL), lambda i: (i, 0))
    gate = jax.ShapeDtypeStruct((n_tok, D_MODEL), jnp.bfloat16)
    branch = jax.ShapeDtypeStruct((n_tok, D_MODEL), jnp.bfloat16)
    return pl.pallas_call(
        functools.partial(_gates_kernel, tiles_per_seq=seq_len // tm),
        grid=(n_tok // tm,),
        in_specs=[
            row_spec,
            pl.BlockSpec((HALO, D_MODEL),
                         lambda i: (jnp.maximum(i * halo_blocks - 1, 0), 0)),
            pl.BlockSpec((HALO, D_MODEL),
                         lambda i: (jnp.minimum((i + 1) * halo_blocks, last_halo), 0)),
            _full(w_rest.shape),
            _full(b_rest.shape),
            _full(conv_w.shape),
            _full(conv_b.shape),
            _full(w_conv_out.shape),
        ],
        out_specs=[row_spec, row_spec, row_spec],
        out_shape=[gate, gate, branch],
        compiler_params=_params(("parallel",)),
        name="gates",
    )(x2, x2, x2, w_rest, b_rest, conv_w, conv_b, w_conv_out)


def _scores(k_ref, qt_ref, scale, u, j, h, kv_tile):
    start = pl.multiple_of(j * kv_tile, kv_tile)
    s = jnp.dot(k_ref[pl.ds(start, kv_tile), :],
                qt_ref[u, h * SPLIT_DEPTH:(h + 1) * SPLIT_DEPTH, :],
                preferred_element_type=jnp.float32)
    return s * scale


def _sublane_partial_sum(e):
    return jnp.sum(e.reshape(e.shape[0] // SUBLANES, SUBLANES, e.shape[1]), axis=0)


def _store_attention(acc, row_sum, o_ref, u):
    tq = acc[0].shape[1]
    o = [a / jnp.sum(l, axis=0, keepdims=True) for a, l in zip(acc, row_sum)]
    rows = pl.ds(pl.multiple_of(u * tq, tq), tq)
    o_ref[rows, :] = jnp.concatenate(o, axis=0).T.astype(o_ref.dtype)


def _attn_bounded_kernel(scale_ref, qt_ref, k_ref, vt_ref, o_ref, p_ref, part_ref, acc_ref,
                         l_ref, *, kv_tile, pass_items):
    n_sub, _, tq = qt_ref.shape
    n_kv = k_ref.shape[0] // kv_tile
    scale = scale_ref[0]
    heads = range(GQA_GROUP)
    pass_items = min(pass_items, n_sub * n_kv)
    if n_kv >= pass_items:
        passes = n_kv // pass_items
        n_pass = n_sub * passes

        def item(t, i):
            return t // passes, (t % passes) * pass_items + i

        def starts_sub_tile(t, i):
            return (t >= 0 if passes == 1 else t % passes == 0) if i == 0 else False
    else:
        per_pass = pass_items // n_kv
        n_pass = n_sub // per_pass

        def item(t, i):
            return t * per_pass + i // n_kv, i % n_kv

        def starts_sub_tile(t, i):
            return i % n_kv == 0

    def probs(u, j, h):
        e = jnp.exp2(_scores(k_ref, qt_ref, scale, u, j, h, kv_tile))
        return e.astype(jnp.bfloat16), _sublane_partial_sum(e)

    for h in heads:
        p_ref[h], part_ref[h] = probs(0, 0, h)
        acc_ref[h] = jnp.zeros(acc_ref.shape[1:], jnp.float32)
        l_ref[h] = jnp.ones(l_ref.shape[1:], jnp.float32)

    def one_pass(t, carry):
        t_next = jnp.minimum(t + 1, n_pass - 1)
        acc = [acc_ref[h] for h in heads]
        row_sum = [l_ref[h] for h in heads]
        u_prev, _ = item(jnp.maximum(t - 1, 0), pass_items - 1)
        _store_attention(acc, row_sum, o_ref, u_prev)
        fresh = starts_sub_tile(t, 0)
        acc = [jnp.where(fresh, 0.0, a) for a in acc]
        row_sum = [jnp.where(fresh, 0.0, l) + part_ref[h] for h, l in zip(heads, row_sum)]
        p = [p_ref[h] for h in heads]
        for i in range(pass_items):
            u, j = item(t, i)
            last = i + 1 == pass_items
            u_ahead, j_ahead = item(t_next, 0) if last else item(t, i + 1)
            switch = (not last) and starts_sub_tile(t, i + 1)
            p_next, row_sum_next = [], []
            for h in heads:
                p_h, part = probs(u_ahead, j_ahead, h)
                p_next.append(p_h)
                if last:
                    part_ref[h] = part
                elif switch:
                    row_sum_next.append(part)
                else:
                    row_sum[h] = row_sum[h] + part
                acc[h] = acc[h] + jnp.dot(vt_ref[j], p[h], preferred_element_type=jnp.float32)
            p = p_next
            if switch:
                _store_attention(acc, row_sum, o_ref, u)
                acc = [jnp.zeros_like(a) for a in acc]
                row_sum = row_sum_next
        for h in heads:
            p_ref[h] = p[h]
            acc_ref[h] = acc[h]
            l_ref[h] = row_sum[h]
        return carry

    lax.fori_loop(0, n_pass, one_pass, 0)
    _store_attention([acc_ref[h] for h in heads], [l_ref[h] for h in heads], o_ref, n_sub - 1)


def _attn_online_kernel(scale_ref, qt_ref, k_ref, vt_ref, o_ref, acc_ref, l_ref, m_ref, *,
                        kv_tile):
    n_sub, _, tq = qt_ref.shape
    n_kv = k_ref.shape[0] // kv_tile
    unroll = min(KV_UNROLL, n_kv)
    scale = scale_ref[0]
    cols = [slice(h * tq, (h + 1) * tq) for h in range(GQA_GROUP)]

    def sub_tile(u, carry):
        m_ref[...] = jnp.full(m_ref.shape, NEG_BIG, jnp.float32)
        acc_ref[...] = jnp.zeros(acc_ref.shape, jnp.float32)
        l_ref[...] = jnp.zeros(l_ref.shape, jnp.float32)

        def scores(j):
            s = jnp.concatenate([_scores(k_ref, qt_ref, scale, u, j, h, kv_tile)
                                 for h in range(GQA_GROUP)], axis=1)
            return s, jnp.max(s, axis=0, keepdims=True)

        def trip(jj, inner):
            base = jj * unroll
            s, s_max = scores(base)
            for t in range(unroll):
                if t + 1 < unroll:
                    s_next, s_max_next = scores(base + t + 1)
                m_prev = m_ref[...]
                m_new = jnp.maximum(m_prev, s_max)
                e = jnp.exp2(s - m_new)
                pv = jnp.dot(vt_ref[base + t], e.astype(jnp.bfloat16),
                             preferred_element_type=jnp.float32)
                alpha = jnp.exp2(m_prev - m_new)
                acc_ref[...] = alpha * acc_ref[...] + pv
                l_ref[...] = alpha * l_ref[...] + _sublane_partial_sum(e)
                m_ref[...] = m_new
                if t + 1 < unroll:
                    s, s_max = s_next, s_max_next
            return inner

        lax.fori_loop(0, n_kv // unroll, trip, 0)
        _store_attention([acc_ref[:, c] for c in cols], [l_ref[:, c] for c in cols], o_ref, u)
        return carry

    lax.fori_loop(0, n_sub, sub_tile, 0)


def _attn_call(score_scale, qt, k, vt, *, bounded, batch, seq_len, tq, q_block, kv_tile):
    n_tok = batch * seq_len
    nq = seq_len // q_block
    n_sub = q_block // tq
    kv_blocks = seq_len // kv_tile
    width = GQA_GROUP * tq
    if bounded:
        body = functools.partial(_attn_bounded_kernel, kv_tile=kv_tile,
                                 pass_items=min(PASS_ITEMS, kv_blocks))
        scratch = [pltpu.VMEM((GQA_GROUP, kv_tile, tq), jnp.bfloat16),
                   pltpu.VMEM((GQA_GROUP, SUBLANES, tq), jnp.float32),
                   pltpu.VMEM((GQA_GROUP, HEAD_DIM, tq), jnp.float32),
                   pltpu.VMEM((GQA_GROUP, SUBLANES, tq), jnp.float32)]
    else:
        body = functools.partial(_attn_online_kernel, kv_tile=kv_tile)
        scratch = [pltpu.VMEM((HEAD_DIM, width), jnp.float32),
                   pltpu.VMEM((SUBLANES, width), jnp.float32),
                   pltpu.VMEM((1, width), jnp.float32)]
    return pl.pallas_call(
        body,
        grid=(batch, N_KV_HEADS, nq),
        in_specs=[
            pl.BlockSpec(memory_space=pltpu.SMEM),
            pl.BlockSpec((n_sub, GQA_GROUP * SPLIT_DEPTH, tq),
                         lambda b, g, i: (b * nq + i, g, 0)),
            pl.BlockSpec((seq_len, SPLIT_DEPTH), lambda b, g, i: (b, g)),
            pl.BlockSpec((None, kv_blocks, HEAD_DIM, kv_tile), lambda b, g, i: (g, b, 0, 0)),
        ],
        out_specs=pl.BlockSpec((q_block, GQA_GROUP * HEAD_DIM),
                               lambda b, g, i: (b * nq + i, g)),
        out_shape=jax.ShapeDtypeStruct((n_tok, D_MODEL), jnp.bfloat16),
        scratch_shapes=scratch,
        compiler_params=_params(("parallel", "parallel", "arbitrary")),
        name="attn_bounded" if bounded else "attn_online",
    )(score_scale, qt, k, vt)


def _out_kernel(x_ref, attn_ref, sz_ref, sga_ref, cg_ref, wao_ref, wo_ref, g_ref, b_ref,
                y_ref):
    f32 = jnp.float32
    a_in = (attn_ref[...].astype(f32) * sz_ref[...].astype(f32)).astype(jnp.bfloat16)
    a_out = jnp.dot(a_in, wao_ref[...], preferred_element_type=jnp.float32)
    merged = sga_ref[...].astype(f32) * a_out + cg_ref[...].astype(f32)
    out = jnp.dot(merged.astype(jnp.bfloat16), wo_ref[...],
                  preferred_element_type=jnp.float32)
    r = DN_ALPHA * x_ref[...] + out
    mu = jnp.mean(r, axis=-1, keepdims=True)
    d = r - mu
    var = jnp.mean(d * d, axis=-1, keepdims=True)
    y_ref[...] = d * lax.rsqrt(var + LN_EPS) * g_ref[...] + b_ref[...]


def _out_call(x2, attn, sz, sga, cg, w_attn_out, w_o, ln_g, ln_b, *, tm):
    n_tok = x2.shape[0]
    row_spec = pl.BlockSpec((tm, D_MODEL), lambda i: (i, 0))
    return pl.pallas_call(
        _out_kernel,
        grid=(n_tok // tm,),
        in_specs=[row_spec] * 5 + [_full(w_attn_out.shape), _full(w_o.shape),
                                    _full(ln_g.shape), _full(ln_b.shape)],
        out_specs=row_spec,
        out_shape=jax.ShapeDtypeStruct((n_tok, D_MODEL), jnp.float32),
        compiler_params=_params(("parallel",)),
        name="out",
    )(x2, attn, sz, sga, cg, w_attn_out, w_o, ln_g, ln_b)


def _axial_tables_t(seq_len):
    rows = seq_len // GRID_W
    inv_freq = ROPE_THETA ** (-jnp.arange(0, ROPE_HALF, 2, dtype=jnp.float32) / ROPE_HALF)
    ang_r = inv_freq[:, None] * jnp.arange(rows, dtype=jnp.float32)
    ang_c = inv_freq[:, None] * jnp.arange(GRID_W, dtype=jnp.float32)
    by_row = lambda t: jnp.repeat(t, GRID_W, axis=1)
    by_col = lambda t: jnp.tile(t, (1, rows))
    return (by_row(jnp.cos(ang_r)), by_row(jnp.sin(ang_r)),
            by_col(jnp.cos(ang_c)), by_col(jnp.sin(ang_c)))


def _tile(n, want):
    t = min(n, want)
    assert n % t == 0, (n, t)
    return t


def _layer(x, p):
    batch, seq_len, _ = x.shape
    x2 = x.reshape(batch * seq_len, D_MODEL)
    tm = _tile(seq_len, TOKEN_TILE)
    tq = _tile(seq_len, Q_TILE)
    kv_tile = _tile(seq_len, min(KV_TILE, max(seq_len // KV_UNROLL, Q_TILE)))
    tabs = _axial_tables_t(seq_len)
    qt, k, vt = _qkv_call(x2, p["w_qkv_t"], p["b_qkv"], p["q_gain"], p["k_gain"], tabs,
                          seq_len=seq_len, tm=max(tm, kv_tile), kv_tile=kv_tile, chunk=tq)
    sz, sga, cg = _gates_call(x2, p["w_rest"], p["b_rest"], p["conv_w"], p["conv_b"],
                              p["w_conv_out"], seq_len=seq_len, tm=tm)
    attend = functools.partial(_attn_call, batch=batch, seq_len=seq_len, tq=tq,
                               q_block=_tile(seq_len, Q_BLOCK), kv_tile=kv_tile)
    score_scale = p["score_scale"].reshape(1)
    attn = lax.cond(p["logit_bound"] <= SAFE_LOGIT_BOUND,
                    functools.partial(attend, bounded=True),
                    functools.partial(attend, bounded=False), score_scale, qt, k, vt)
    y = _out_call(x2, attn, sz, sga, cg, p["w_attn_out"], p["w_o"], p["ln_g"], p["ln_b"],
                  tm=tm)
    return y.reshape(batch, seq_len, D_MODEL)


def _prepare(w_in, b_in, q_gain, k_gain, conv_w, conv_b, w_attn_out, w_conv_out, w_o,
             ln_g, ln_b):
    bf = jnp.bfloat16
    tiny = jnp.finfo(jnp.float32).tiny
    q_peak = jnp.maximum(jnp.max(jnp.abs(q_gain)), tiny)
    k_peak = jnp.maximum(jnp.max(jnp.abs(k_gain)), tiny)
    return {
        "w_qkv_t": w_in[:, :QKV_WIDTH].T.astype(bf),
        "b_qkv": b_in[:QKV_WIDTH].reshape(QKV_WIDTH, 1),
        "q_gain": (q_gain * (OPERAND_PEAK / q_peak)).reshape(HEAD_DIM, 1),
        "k_gain": (k_gain * (OPERAND_PEAK / k_peak)).reshape(HEAD_DIM, 1),
        "score_scale": (SCORE_SCALE * LOG2_E / OPERAND_PEAK ** 2) * q_peak * k_peak,
        "logit_bound": math.sqrt(HEAD_DIM) * q_peak * k_peak,
        "w_rest": w_in[:, QKV_WIDTH:].astype(bf),
        "b_rest": b_in[QKV_WIDTH:].reshape(1, -1),
        "conv_w": conv_w,
        "conv_b": conv_b.reshape(1, D_MODEL),
        "w_conv_out": w_conv_out.astype(bf),
        "w_attn_out": w_attn_out.astype(bf),
        "w_o": w_o.astype(bf),
        "ln_g": ln_g.reshape(1, D_MODEL),
        "ln_b": ln_b.reshape(1, D_MODEL),
    }


def kernel(x_prompt, x_sample, w_in, b_in, q_gain, k_gain, conv_w, conv_b, w_attn_out,
           w_conv_out, w_o, ln_g, ln_b):
    depth = w_in.shape[0]
    y_prompt, y_sample = x_prompt, x_sample
    for l in range(depth):
        p = _prepare(w_in[l], b_in[l], q_gain[l], k_gain[l], conv_w[l], conv_b[l],
                     w_attn_out[l], w_conv_out[l], w_o[l], ln_g[l], ln_b[l])
        y_prompt = _layer(y_prompt, p)
        y_sample = _layer(y_sample, p)
    return (y_prompt, y_sample)
```
